```python
import math
import jax, jax.numpy as jnp
from jax import lax
import numpy as np

D_MODEL = 1024
BATCH = 8
SEQ = 2048
DEPTH = 4
DEC_BATCH = 128
DEC_SEQ = 4
PAST_LEN = 16384
PAGE_SIZE = 128

N_META = 16
D_MIX = D_MODEL
LRU_WIDTH = D_MIX // 2
LRU_HEADS = 8
LRU_HEAD_DIM = LRU_WIDTH // LRU_HEADS
LRU_C = 8.0
CONV_K = 4
SSD_INNER = D_MIX - LRU_WIDTH
SSD_HEAD_DIM = 64
SSD_HEADS = SSD_INNER // SSD_HEAD_DIM
SSD_GROUPS = 2
SSD_STATE = 128
SSD_CONV_DIM = SSD_INNER + 2 * SSD_GROUPS * SSD_STATE
SSD_CHUNK = 128
D_FF = 3 * D_MODEL
FFN_CONV_K = 3
P_IN = 2 * LRU_WIDTH + SSD_INNER + SSD_CONV_DIM + SSD_HEADS
EPS = 1e-6

kernel_name = "hymba_rglru_ssd_convffn_step"


def rmsnorm(x, g):
    xf = x.astype(jnp.float32)
    y = xf * lax.rsqrt(jnp.mean(xf * xf, axis=-1, keepdims=True) + EPS)
    return (y * g.astype(jnp.float32)).astype(x.dtype)


def causal_dwconv(x, buf, w, b):
    K = w.shape[0]
    L = x.shape[1]
    xp = jnp.concatenate([buf.astype(x.dtype), x], axis=1)
    y = b + sum(xp[:, k:k + L] * w[k] for k in range(K))
    return y, xp[:, L:]


def rg_lru(x, h0, wa, ba, wx, bx, a_param):
    Bsz, L, W = x.shape
    xh = x.reshape(Bsz, L, LRU_HEADS, LRU_HEAD_DIM)
    r = jax.nn.sigmoid(jnp.einsum('blhi,hij->blhj', xh, wa).reshape(Bsz, L, W) + ba)
    i = jax.nn.sigmoid(jnp.einsum('blhi,hij->blhj', xh, wx).reshape(Bsz, L, W) + bx)
    log_a = -LRU_C * r.astype(jnp.float32) * jax.nn.softplus(-a_param.astype(jnp.float32))
    a = jnp.exp(log_a)
    mult = jnp.sqrt(-jnp.expm1(2.0 * log_a))
    b = mult * (i * x).astype(jnp.float32)
    b = b.at[:, 0].add(a[:, 0] * h0.astype(jnp.float32))

    def combine(lft, rgt):
        a_l, b_l = lft
        a_r, b_r = rgt
        return a_l * a_r, a_r * b_l + b_r

    _, h = lax.associative_scan(combine, (a, b), axis=1)
    return h.astype(x.dtype), h[:, -1]


def ssd_chunked(x, dt, A, Bm, Cm, h0):
    Bsz, L, H, P = x.shape
    Q = min(SSD_CHUNK, L)
    pad = (-L) % Q
    xf = x.astype(jnp.float32)
    Bf = Bm.astype(jnp.float32)
    Cf = Cm.astype(jnp.float32)
    if pad:
        pw = ((0, 0), (0, pad))
        xf = jnp.pad(xf, pw + ((0, 0), (0, 0)))
        dt = jnp.pad(dt, pw + ((0, 0),))
        Bf = jnp.pad(Bf, pw + ((0, 0), (0, 0)))
        Cf = jnp.pad(Cf, pw + ((0, 0), (0, 0)))
    nC = (L + pad) // Q
    rep = H // SSD_GROUPS
    xc = xf.reshape(Bsz, nC, Q, H, P)
    dtc = dt.reshape(Bsz, nC, Q, H)
    Bh = jnp.repeat(Bf.reshape(Bsz, nC, Q, SSD_GROUPS, -1), rep, axis=3)
    Ch = jnp.repeat(Cf.reshape(Bsz, nC, Q, SSD_GROUPS, -1), rep, axis=3)
    cs = jnp.cumsum(dtc * A.astype(jnp.float32), axis=2)
    mask = jnp.tril(jnp.ones((Q, Q), dtype=bool))[None, None, :, :, None]
    seg = cs[:, :, :, None, :] - cs[:, :, None, :, :]
    Lmat = jnp.exp(jnp.where(mask, seg, -jnp.inf))
    xdt = xc * dtc[..., None]
    scores = jnp.einsum('bcthn,bcshn->bctsh', Ch, Bh) * Lmat
    y_intra = jnp.einsum('bctsh,bcshp->bcthp', scores, xdt)
    decay_to_end = jnp.exp(cs[:, :, -1:, :] - cs)
    chunk_states = jnp.einsum('bcsh,bcshn,bcshp->bchpn', decay_to_end, Bh, xdt)
    chunk_decay = jnp.exp(cs[:, :, -1, :])

    def step(h, inp):
        dec, st = inp
        return h * dec[..., None, None] + st, h

    hT, h_in = lax.scan(step, h0.astype(jnp.float32),
                        (jnp.swapaxes(chunk_decay, 0, 1), jnp.swapaxes(chunk_states, 0, 1)))
    h_in = jnp.swapaxes(h_in, 0, 1)
    y_inter = jnp.einsum('bcthn,bchpn->bcthp', Ch, h_in) * jnp.exp(cs)[..., None]
    y = (y_intra + y_inter).reshape(Bsz, nC * Q, H, P)[:, :L]
    return y.astype(x.dtype), hT


def mixer(h, n_lead, st_lru_conv, st_lru_h, st_ssd_conv, st_ssd, p, l):
    Bsz, L, _ = h.shape
    proj = h @ p['w_in'][l]
    o1 = LRU_WIDTH
    o2 = o1 + LRU_WIDTH
    o3 = o2 + SSD_INNER
    o4 = o3 + SSD_CONV_DIM
    lru_x, lru_gate, z, xbc, dt_raw = (proj[..., :o1], proj[..., o1:o2], proj[..., o2:o3],
                                       proj[..., o3:o4], proj[..., o4:])
    xc, new_lru_conv = causal_dwconv(lru_x, st_lru_conv, p['lru_conv_w'][l], p['lru_conv_b'][l])
    hl, new_lru_h = rg_lru(xc, st_lru_h, p['lru_wa'][l], p['lru_ba'][l], p['lru_wx'][l],
                           p['lru_bx'][l], p['lru_a_param'][l])
    lru_out = rmsnorm(hl * jax.nn.gelu(lru_gate), p['lru_out_norm'][l])
    xbc_c, new_ssd_conv = causal_dwconv(xbc, st_ssd_conv, p['ssd_conv_w'][l], p['ssd_conv_b'][l])
    xbc_c = jax.nn.silu(xbc_c)
    xs = xbc_c[..., :SSD_INNER].reshape(Bsz, L, SSD_HEADS, SSD_HEAD_DIM)
    Bm = xbc_c[..., SSD_INNER:SSD_INNER + SSD_GROUPS * SSD_STATE].reshape(Bsz, L, SSD_GROUPS, SSD_STATE)
    Cm = xbc_c[..., SSD_INNER + SSD_GROUPS * SSD_STATE:].reshape(Bsz, L, SSD_GROUPS, SSD_STATE)
    dt = jax.nn.softplus((dt_raw + p['ssd_dt_bias'][l]).astype(jnp.float32))
    A = -jnp.exp(p['ssd_a_log'][l].astype(jnp.float32))
    if n_lead > 0:
        y1, h1 = ssd_chunked(xs[:, :n_lead], dt[:, :n_lead], A, Bm[:, :n_lead], Cm[:, :n_lead], st_ssd)
        y2, new_ssd = ssd_chunked(xs[:, n_lead:], dt[:, n_lead:], A, Bm[:, n_lead:], Cm[:, n_lead:], h1)
        ys = jnp.concatenate([y1, y2], axis=1)
    else:
        ys, new_ssd = ssd_chunked(xs, dt, A, Bm, Cm, st_ssd)
    ys = ys + xs * p['ssd_d'][l][:, None]
    ys = ys.reshape(Bsz, L, SSD_INNER)
    ssd_out = rmsnorm(ys * jax.nn.silu(z), p['ssd_out_norm'][l])
    out = jnp.concatenate([lru_out, ssd_out], axis=-1) @ p['w_out'][l]
    return out, new_lru_conv, new_lru_h, new_ssd_conv, new_ssd


def conv_ffn(h, buf, w_up, cw, cb, w_down):
    u = h @ w_up
    u, new_buf = causal_dwconv(u, buf, cw, cb)
    g, v = u[..., :D_FF], u[..., D_FF:]
    return (jax.nn.gelu(g) * v) @ w_down, new_buf


def trunk(x, n_lead, st_lru_conv, st_lru_h, st_ssd_conv, st_ssd, st_ffn_conv, p):
    o_lc, o_lh, o_sc, o_ss, o_fc = [], [], [], [], []
    for l in range(DEPTH):
        h = rmsnorm(x, p['norm_mix'][l])
        m, lc, lh, sc, ss = mixer(h, n_lead, st_lru_conv[l], st_lru_h[l], st_ssd_conv[l], st_ssd[l], p, l)
        x = x + m
        h = rmsnorm(x, p['norm_ffn'][l])
        f, fc = conv_ffn(h, st_ffn_conv[l], p['ffn_w_up'][l], p['ffn_conv_w'][l],
                         p['ffn_conv_b'][l], p['ffn_w_down'][l])
        x = x + f
        o_lc.append(lc); o_lh.append(lh); o_sc.append(sc); o_ss.append(ss); o_fc.append(fc)
    y = rmsnorm(x, p['norm_final'])
    return y, jnp.stack(o_lc), jnp.stack(o_lh), jnp.stack(o_sc), jnp.stack(o_ss), jnp.stack(o_fc)


def setup_inputs(seed: int = 0) -> dict:
    key = jax.random.key(seed)
    ks = iter(jax.random.split(key, 40))
    nrm = lambda shape, s: jax.random.normal(next(ks), shape, jnp.float32) * s
    gain = lambda shape: 1.0 + nrm(shape, 0.02)
    a0 = jax.random.uniform(next(ks), (DEPTH, LRU_WIDTH), jnp.float32, 0.9, 0.999) ** (1.0 / LRU_C)
    lru_a_param = jnp.log(a0) - jnp.log1p(-a0)
    dt0 = jnp.exp(jax.random.uniform(next(ks), (DEPTH, SSD_HEADS), jnp.float32, math.log(1e-3), math.log(1e-1)))
    ssd_dt_bias = dt0 + jnp.log(-jnp.expm1(-dt0))
    ssd_a_log = jnp.log(jax.random.uniform(next(ks), (DEPTH, SSD_HEADS), jnp.float32, 1.0, 16.0))
    return {
        'x_prompt': nrm((BATCH, SEQ, D_MODEL), 1.0),
        'x_sample': nrm((DEC_BATCH, DEC_SEQ, D_MODEL), 1.0),
        'state_lru_conv': nrm((DEPTH, DEC_BATCH, CONV_K - 1, LRU_WIDTH), 1.0),
        'state_lru_h': nrm((DEPTH, DEC_BATCH, LRU_WIDTH), 0.5),
        'state_ssd_conv': nrm((DEPTH, DEC_BATCH, CONV_K - 1, SSD_CONV_DIM), 1.0),
        'state_ssd': nrm((DEPTH, DEC_BATCH, SSD_HEADS, SSD_HEAD_DIM, SSD_STATE), 0.3),
        'state_ffn_conv': nrm((DEPTH, DEC_BATCH, FFN_CONV_K - 1, 2 * D_FF), 1.0),
        'meta_tokens': nrm((N_META, D_MODEL), 1.0),
        'norm_mix': gain((DEPTH, D_MODEL)),
        'w_in': nrm((DEPTH, D_MODEL, P_IN), D_MODEL ** -0.5),
        'lru_conv_w': nrm((DEPTH, CONV_K, LRU_WIDTH), CONV_K ** -0.5),
        'lru_conv_b': nrm((DEPTH, LRU_WIDTH), 0.02),
        'lru_wa': nrm((DEPTH, LRU_HEADS, LRU_HEAD_DIM, LRU_HEAD_DIM), LRU_HEAD_DIM ** -0.5),
        'lru_ba': nrm((DEPTH, LRU_WIDTH), 0.02),
        'lru_wx': nrm((DEPTH, LRU_HEADS, LRU_HEAD_DIM, LRU_HEAD_DIM), LRU_HEAD_DIM ** -0.5),
        'lru_bx': nrm((DEPTH, LRU_WIDTH), 0.02),
        'lru_a_param': lru_a_param,
        'lru_out_norm': gain((DEPTH, LRU_WIDTH)),
        'ssd_conv_w': nrm((DEPTH, CONV_K, SSD_CONV_DIM), CONV_K ** -0.5),
        'ssd_conv_b': nrm((DEPTH, SSD_CONV_DIM), 0.02),
        'ssd_dt_bias': ssd_dt_bias,
        'ssd_a_log': ssd_a_log,
        'ssd_d': gain((DEPTH, SSD_HEADS)),
        'ssd_out_norm': gain((DEPTH, SSD_INNER)),
        'w_out': nrm((DEPTH, D_MIX, D_MODEL), D_MIX ** -0.5),
        'norm_ffn': gain((DEPTH, D_MODEL)),
        'ffn_w_up': nrm((DEPTH, D_MODEL, 2 * D_FF), D_MODEL ** -0.5),
        'ffn_conv_w': nrm((DEPTH, FFN_CONV_K, 2 * D_FF), FFN_CONV_K ** -0.5),
        'ffn_conv_b': nrm((DEPTH, 2 * D_FF), 0.02),
        'ffn_w_down': nrm((DEPTH, D_FF, D_MODEL), D_FF ** -0.5),
        'norm_final': gain((D_MODEL,)),
    }


def reference(x_prompt, x_sample, state_lru_conv, state_lru_h, state_ssd_conv, state_ssd, state_ffn_conv,
              meta_tokens, norm_mix, w_in, lru_conv_w, lru_conv_b, lru_wa, lru_ba, lru_wx, lru_bx,
              lru_a_param, lru_out_norm, ssd_conv_w, ssd_conv_b, ssd_dt_bias, ssd_a_log, ssd_d,
              ssd_out_norm, w_out, norm_ffn, ffn_w_up, ffn_conv_w, ffn_conv_b, ffn_w_down, norm_final):
    p = dict(norm_mix=norm_mix, w_in=w_in, lru_conv_w=lru_conv_w, lru_conv_b=lru_conv_b,
             lru_wa=lru_wa, lru_ba=lru_ba, lru_wx=lru_wx, lru_bx=lru_bx, lru_a_param=lru_a_param,
             lru_out_norm=lru_out_norm, ssd_conv_w=ssd_conv_w, ssd_conv_b=ssd_conv_b,
             ssd_dt_bias=ssd_dt_bias, ssd_a_log=ssd_a_log, ssd_d=ssd_d, ssd_out_norm=ssd_out_norm,
             w_out=w_out, norm_ffn=norm_ffn, ffn_w_up=ffn_w_up, ffn_conv_w=ffn_conv_w,
             ffn_conv_b=ffn_conv_b, ffn_w_down=ffn_w_down, norm_final=norm_final)
    bp = x_prompt.shape[0]
    dtp = x_prompt.dtype
    xp = jnp.concatenate([jnp.broadcast_to(meta_tokens.astype(dtp)[None], (bp, N_META, D_MODEL)), x_prompt], axis=1)
    z_lc = jnp.zeros((DEPTH, bp, CONV_K - 1, LRU_WIDTH), dtp)
    z_lh = jnp.zeros((DEPTH, bp, LRU_WIDTH), jnp.float32)
    z_sc = jnp.zeros((DEPTH, bp, CONV_K - 1, SSD_CONV_DIM), dtp)
    z_ss = jnp.zeros((DEPTH, bp, SSD_HEADS, SSD_HEAD_DIM, SSD_STATE), jnp.float32)
    z_fc = jnp.zeros((DEPTH, bp, FFN_CONV_K - 1, 2 * D_FF), dtp)
    yp, p_lru_conv, p_lru_h, p_ssd_conv, p_ssd, p_ffn_conv = trunk(xp, N_META, z_lc, z_lh, z_sc, z_ss, z_fc, p)
    y_prompt = yp[:, N_META:]
    y_sample, s_lru_conv, s_lru_h, s_ssd_conv, s_ssd, s_ffn_conv = trunk(
        x_sample, 0, state_lru_conv, state_lru_h, state_ssd_conv, state_ssd, state_ffn_conv, p)
    return (y_prompt, y_sample, p_lru_conv, p_lru_h, p_ssd_conv, p_ssd, p_ffn_conv,
            s_lru_conv, s_lru_h, s_ssd_conv, s_ssd, s_ffn_conv)
```

```python
import functools

import jax
import jax.numpy as jnp
from jax import lax
from jax.experimental import pallas as pl
from jax.experimental.pallas import tpu as pltpu

f32 = jnp.float32
bf16 = jnp.bfloat16

EPS = 1e-6
LRU_C = 8.0
HEADS = 8
HEAD_P = 64
SSD_N = 128
CHUNK = 128
LANES = 128
SUBLANES = 8
VMEM_LIMIT = 56 * 1024 * 1024
HI = lax.Precision.HIGHEST


def _rms(x, g):
    return x * lax.rsqrt(jnp.mean(x * x, axis=-1, keepdims=True) + EPS) * g


def _silu(x):
    return x * jax.nn.sigmoid(x)


def _softplus(x):
    return jnp.maximum(x, 0.0) + jnp.log1p(jnp.exp(-jnp.abs(x)))


def _dot(a, b):
    return jnp.dot(a, b, preferred_element_type=f32)


def _lru_coeffs(xc, pre_r, pre_i, a_param):
    r = jax.nn.sigmoid(pre_r)
    i = jax.nn.sigmoid(pre_i)
    log_a = -LRU_C * r * _softplus(-a_param)
    a = jnp.exp(log_a)
    th = jnp.tanh(log_a)
    mult = jnp.sqrt(-2.0 * th / (1.0 - th))
    return a, mult * (i * xc)


def _gates(xc, wg_ref, bg_ref):
    half = wg_ref.shape[1]
    pr, pi = [], []
    for j in range(wg_ref.shape[0]):
        ri = _dot(xc[:, j * half:(j + 1) * half].astype(bf16), wg_ref[j])
        pr.append(ri[:, :half])
        pi.append(ri[:, half:])
    bg = bg_ref[...]
    return jnp.concatenate(pr, axis=1) + bg[0:1], jnp.concatenate(pi, axis=1) + bg[1:2]


def _head_expand():
    r = lax.broadcasted_iota(jnp.int32, (LANES, HEADS * HEAD_P), 0)
    c = lax.broadcasted_iota(jnp.int32, (LANES, HEADS * HEAD_P), 1)
    return (r == lax.shift_right_logical(c, 6)).astype(f32)


def _scan_rows(a, b, h0):
    T, C = a.shape
    G = T // SUBLANES
    a3 = a.reshape(G, SUBLANES, C)
    b3 = b.reshape(G, SUBLANES, C)
    sub = lax.broadcasted_iota(jnp.int32, a3.shape, 1)
    for d in (1, 2, 4):
        ar = pltpu.roll(a3, d, axis=1)
        br = pltpu.roll(b3, d, axis=1)
        m = sub >= d
        b3 = jnp.where(m, a3 * br + b3, b3)
        a3 = jnp.where(m, a3 * ar, a3)
    hs = []
    hp = h0
    for gi in range(G):
        hb = a3[gi] * hp + b3[gi]
        hs.append(hb)
        hp = hb[SUBLANES - 1:SUBLANES, :]
    return jnp.concatenate(hs, axis=0), hp


def _p_mixer_kernel(x_ref, gmix_ref, win_ref, lcw_ref, lcb_ref, wg_ref, bg_ref, ap_ref, lnorm_ref,
                    scw_ref, scb_ref, dtb_ref, alog_ref, dvec_ref, snorm_ref, wout_ref,
                    st_lc_ref, st_lh_ref, st_sc_ref, st_ss_ref,
                    o_ref, n_lc_ref, n_lh_ref, n_sc_ref, n_ss_ref,
                    ext_l, ext_s, hcar, s_pair, ys_sc, xb_sc, dt_sc, *, T):
    t = pl.program_id(1)
    nt = pl.num_programs(1)
    W = ext_l.shape[1]
    CD = ext_s.shape[1]
    SI = HEADS * HEAD_P
    NP = HEADS // 2

    @pl.when(t == 0)
    def _():
        ext_l[5:8, :] = st_lc_ref[0]
        ext_s[5:8, :] = st_sc_ref[0]
        hcar[0:1, :] = st_lh_ref[0]
        for j in range(NP):
            s_pair[j] = st_ss_ref[0, j * LANES:(j + 1) * LANES, :].T

    x = x_ref[0]
    hn = _rms(x, gmix_ref[...]).astype(bf16)

    ext_l[8:T + 8, :] = _dot(hn, win_ref[:, CD:CD + W])
    lcw = lcw_ref[...]
    xc = lcb_ref[...] + lcw[3:4] * ext_l[8:T + 8, :]
    for k in range(3):
        xc = xc + lcw[k:k + 1] * ext_l[5 + k:5 + k + T, :]
    ext_l[5:8, :] = ext_l[T + 5:T + 8, :]
    pre_r, pre_i = _gates(xc, wg_ref, bg_ref)
    a, bt = _lru_coeffs(xc, pre_r, pre_i, ap_ref[...])
    hl, hlast = _scan_rows(a, bt, hcar[0:1, :])
    hcar[0:1, :] = hlast
    gate = _dot(hn, win_ref[:, CD + W:CD + 2 * W])
    lru_out = _rms(hl * jax.nn.gelu(gate), lnorm_ref[...]).astype(bf16)

    ext_s[8:T + 8, :] = _dot(hn, win_ref[:, 0:CD])
    scw = scw_ref[...]
    xb = scb_ref[...] + scw[3:4] * ext_s[8:T + 8, :]
    for k in range(3):
        xb = xb + scw[k:k + 1] * ext_s[5 + k:5 + k + T, :]
    ext_s[5:8, :] = ext_s[T + 5:T + 8, :]
    xb_sc[...] = _silu(xb)
    dt_sc[...] = _softplus(_dot(hn, win_ref[:, CD + 3 * W:CD + 3 * W + LANES]) + dtb_ref[...])

    a_row = -jnp.exp(alog_ref[...])
    row = lax.broadcasted_iota(jnp.int32, (CHUNK, CHUNK), 0)
    col = lax.broadcasted_iota(jnp.int32, (CHUNK, CHUNK), 1)
    causal = col <= row
    ltri = causal.astype(f32)
    lane_l = col < HEAD_P
    eexp = _head_expand()
    dvec = dvec_ref[...]
    for q in range(T // CHUNK):
        r0 = q * CHUNK
        xs_c = xb_sc[r0:r0 + CHUNK, 0:SI]
        dt_c = dt_sc[r0:r0 + CHUNK, :]
        cs_col = jnp.dot(ltri, dt_c * a_row, precision=HI, preferred_element_type=f32)
        cs_row = cs_col.T[0:HEADS]
        dt_row = dt_c.T[0:HEADS]
        ecol = jnp.exp(cs_col)
        w8 = jnp.exp(cs_row[:, CHUNK - 1:CHUNK] - cs_row) * dt_row
        dec_x = jnp.exp(jnp.dot(cs_col[CHUNK - SUBLANES:CHUNK], eexp, precision=HI,
                                preferred_element_type=f32)[SUBLANES - 1:SUBLANES])
        for g in range(2):
            bm_t = xb_sc[r0:r0 + CHUNK, SI + g * SSD_N:SI + (g + 1) * SSD_N].T
            cm = xb_sc[r0:r0 + CHUNK, SI + 2 * SSD_N + g * SSD_N:SI + 2 * SSD_N + (g + 1) * SSD_N]
            scores = _dot(cm.astype(bf16), bm_t.astype(bf16))
            for jj in range(2):
                j = 2 * g + jj
                lhs, lhs2 = [], []
                for h in (2 * j, 2 * j + 1):
                    lmat = jnp.where(causal, jnp.exp(cs_col[:, h:h + 1] - cs_row[h:h + 1]), 0.0)
                    lhs.append(scores * lmat * dt_row[h:h + 1])
                    lhs.append(cm * ecol[:, h:h + 1])
                    lhs2.append(bm_t * w8[h:h + 1])
                xp = xs_c[:, j * LANES:(j + 1) * LANES]
                sp = s_pair[j]
                xl = jnp.where(lane_l, xp, 0.0)
                xr = jnp.where(lane_l, 0.0, xp)
                rhs = jnp.concatenate([xl, jnp.where(lane_l, sp, 0.0), xr, jnp.where(lane_l, 0.0, sp)], axis=0)
                y = _dot(jnp.concatenate(lhs, axis=1).astype(bf16), rhs.astype(bf16))
                upd = _dot(jnp.concatenate(lhs2, axis=1).astype(bf16),
                           jnp.concatenate([xl, xr], axis=0).astype(bf16))
                s_pair[j] = dec_x[:, j * LANES:(j + 1) * LANES] * sp + upd
                ys_sc[r0:r0 + CHUNK, j * LANES:(j + 1) * LANES] = y + xp * dvec[:, j * LANES:(j + 1) * LANES]

    z = _dot(hn, win_ref[:, CD + 2 * W:CD + 3 * W])
    ssd_out = _rms(ys_sc[...] * _silu(z), snorm_ref[...]).astype(bf16)
    o_ref[0] = x + _dot(lru_out, wout_ref[0:W, :]) + _dot(ssd_out, wout_ref[W:W + SI, :])

    @pl.when(t == nt - 1)
    def _():
        n_lc_ref[0] = ext_l[5:8, :]
        n_sc_ref[0] = ext_s[5:8, :]
        n_lh_ref[0] = hcar[0:1, :]
        for j in range(NP):
            n_ss_ref[0, j * LANES:(j + 1) * LANES, :] = s_pair[j].T


def _const_spec(shape):
    nd = len(shape)
    return pl.BlockSpec(shape, lambda *_: (0,) * nd, pipeline_mode=pl.Buffered(1))


def _p_mixer(x, w, st, *, T):
    B, S, D = x.shape
    W = w["lcw"].shape[1]
    CD = w["scw"].shape[1]
    SI = HEADS * HEAD_P
    consts = [w["gmix"], w["win"], w["lcw"], w["lcb"], w["wg"], w["bg"], w["ap"], w["lnorm"],
              w["scw"], w["scb"], w["dtb"], w["alog"], w["dvec"], w["snorm"], w["wout"]]
    st_specs = [pl.BlockSpec((1, 3, W), lambda b, t: (b, 0, 0)),
                pl.BlockSpec((1, 1, W), lambda b, t: (b, 0, 0)),
                pl.BlockSpec((1, 3, CD), lambda b, t: (b, 0, 0)),
                pl.BlockSpec((1, SI, SSD_N), lambda b, t: (b, 0, 0))]
    x_spec = pl.BlockSpec((1, T, D), lambda b, t: (b, t, 0))
    return pl.pallas_call(
        functools.partial(_p_mixer_kernel, T=T),
        grid=(B, S // T),
        in_specs=[x_spec] + [_const_spec(c.shape) for c in consts] + st_specs,
        out_specs=[x_spec] + st_specs,
        out_shape=[jax.ShapeDtypeStruct((B, S, D), f32),
                   jax.ShapeDtypeStruct((B, 3, W), f32), jax.ShapeDtypeStruct((B, 1, W), f32),
                   jax.ShapeDtypeStruct((B, 3, CD), f32), jax.ShapeDtypeStruct((B, SI, SSD_N), f32)],
        scratch_shapes=[pltpu.VMEM((T + 8, W), f32), pltpu.VMEM((T + 8, CD), f32),
                        pltpu.VMEM((SUBLANES, W), f32), pltpu.VMEM((HEADS // 2, SSD_N, LANES), f32),
                        pltpu.VMEM((T, SI), f32), pltpu.VMEM((T, CD), f32), pltpu.VMEM((T, LANES), f32)],
        compiler_params=pltpu.CompilerParams(dimension_semantics=("arbitrary", "arbitrary"),
                                             vmem_limit_bytes=VMEM_LIMIT),
        name="p_mixer",
    )(x, *consts, st["lc"], st["lh"], st["sc"], st["ss"])


def _p_ffn_kernel(x_ref, g_ref, wup_ref, cw_ref, cb_ref, wdn_ref, gfin_ref, st_ref,
                  o_ref, nst_ref, ext, carry, *, T, FC, final):
    t = pl.program_id(1)
    nt = pl.num_programs(1)
    F = wdn_ref.shape[0]

    @pl.when(t == 0)
    def _():
        carry[6:8, :] = st_ref[0]

    x = x_ref[0]
    hn = _rms(x, g_ref[...]).astype(bf16)
    acc = jnp.zeros(x.shape, f32)
    for c in range(F // FC):
        ys = []
        for half in range(2):
            c0 = half * F + c * FC
            u = _dot(hn, wup_ref[:, c0:c0 + FC])
            ext[half, 8:T + 8, :] = u
            ext[half, 6:8, :] = carry[6:8, c0:c0 + FC]
            cw = cw_ref[:, c0:c0 + FC]
            ys.append(cb_ref[:, c0:c0 + FC] + cw[0:1] * ext[half, 6:T + 6, :]
                      + cw[1:2] * ext[half, 7:T + 7, :] + cw[2:3] * u)
            carry[6:8, c0:c0 + FC] = ext[half, T + 6:T + 8, :]
        act = (jax.nn.gelu(ys[0]) * ys[1]).astype(bf16)
        acc = acc + _dot(act, wdn_ref[c * FC:(c + 1) * FC, :])
    out = x + acc
    if final:
        out = _rms(out, gfin_ref[...])
    o_ref[0] = out

    @pl.when(t == nt - 1)
    def _():
        nst_ref[0] = carry[6:8, :]


def _p_ffn(x, w, st_fc, gfin, *, T, FC, final):
    B, S, D = x.shape
    F2 = w["wup"].shape[1]
    consts = [w["gffn"], w["wup"], w["cw"], w["cb"], w["wdn"], gfin]
    st_spec = pl.BlockSpec((1, 2, F2), lambda b, t: (b, 0, 0))
    x_spec = pl.BlockSpec((1, T, D), lambda b, t: (b, t, 0))
    return pl.pallas_call(
        functools.partial(_p_ffn_kernel, T=T, FC=FC, final=final),
        grid=(B, S // T),
        in_specs=[x_spec] + [_const_spec(c.shape) for c in consts] + [st_spec],
        out_specs=[x_spec, st_spec],
        out_shape=[jax.ShapeDtypeStruct((B, S, D), f32), jax.ShapeDtypeStruct((B, 2, F2), f32)],
        scratch_shapes=[pltpu.VMEM((2, T + 8, FC), f32), pltpu.VMEM((SUBLANES, F2), f32)],
        compiler_params=pltpu.CompilerParams(dimension_semantics=("arbitrary", "arbitrary"),
                                             vmem_limit_bytes=VMEM_LIMIT),
        name="p_ffn",
    )(x, *consts, st_fc)


def _tm_conv(state, u, w, b, n_rows, L):
    K = w.shape[0]
    ext = jnp.concatenate([state, u], axis=0)
    R = L * n_rows
    y = b
    for k in range(K):
        y = y + w[k:k + 1] * ext[k * n_rows:k * n_rows + R]
    return y, ext[R:R + (K - 1) * n_rows]


def _s_mix_a_kernel(x_ref, gmix_ref, win_ref, lcw_ref, lcb_ref, wg_ref, bg_ref, ap_ref, lnorm_ref,
                    scw_ref, scb_ref, dtb_ref, alog_ref, dvec_ref,
                    st_lc_ref, st_lh_ref, st_sc_ref,
                    lru_ref, z_ref, bc_ref, ypart_ref, ecs_ref, xw_ref, dec_ref,
                    n_lc_ref, n_lh_ref, n_sc_ref, *, L, Bn):
    W = lcw_ref.shape[1]
    CD = scw_ref.shape[1]
    SI = HEADS * HEAD_P
    hn = _rms(x_ref[...], gmix_ref[...]).astype(bf16)

    xc, n_lc = _tm_conv(st_lc_ref[...], _dot(hn, win_ref[:, CD:CD + W]), lcw_ref[...], lcb_ref[...], Bn, L)
    n_lc_ref[...] = n_lc
    pre_r, pre_i = _gates(xc, wg_ref, bg_ref)
    a, bt = _lru_coeffs(xc, pre_r, pre_i, ap_ref[...])
    hp = st_lh_ref[...]
    hs = []
    for t in range(L):
        hp = a[t * Bn:(t + 1) * Bn] * hp + bt[t * Bn:(t + 1) * Bn]
        hs.append(hp)
    n_lh_ref[...] = hp
    gate = _dot(hn, win_ref[:, CD + W:CD + 2 * W])
    lru_ref[...] = _rms(jnp.concatenate(hs, axis=0) * jax.nn.gelu(gate), lnorm_ref[...])
    z_ref[...] = _dot(hn, win_ref[:, CD + 2 * W:CD + 3 * W])

    xb, n_sc = _tm_conv(st_sc_ref[...], _dot(hn, win_ref[:, 0:CD]), scw_ref[...], scb_ref[...], Bn, L)
    n_sc_ref[...] = n_sc
    xb = _silu(xb)
    xs = xb[:, 0:SI]
    bc_ref[...] = xb[:, SI:CD]
    dt = _softplus(_dot(hn, win_ref[:, CD + 3 * W:CD + 3 * W + LANES]) + dtb_ref[...])
    dA = dt * (-jnp.exp(alog_ref[...]))
    css = []
    cs = jnp.zeros((Bn, LANES), f32)
    for t in range(L):
        cs = cs + dA[t * Bn:(t + 1) * Bn]
        css.append(cs)
    dec_ref[...] = jnp.exp(cs)
    eexp = _head_expand()
    csx = jnp.dot(jnp.concatenate(css, axis=0), eexp, precision=HI, preferred_element_type=f32)
    dtx = jnp.dot(dt, eexp, precision=HI, preferred_element_type=f32)
    ecs_ref[...] = jnp.exp(csx)
    cs_end = csx[(L - 1) * Bn:L * Bn]
    dvec = dvec_ref[...]
    half = SI // 2
    for t in range(L):
        rt = slice(t * Bn, (t + 1) * Bn)
        xw_ref[rt, :] = xs[rt] * jnp.exp(cs_end - csx[rt]) * dtx[rt]
        acc = dvec * xs[rt]
        cm = xb[rt, SI + 2 * SSD_N:CD]
        for s in range(t + 1):
            rs = slice(s * Bn, (s + 1) * Bn)
            prod = cm * xb[rs, SI:SI + 2 * SSD_N]
            gx = jnp.concatenate(
                [jnp.broadcast_to(jnp.sum(prod[:, 0:SSD_N], axis=-1, keepdims=True), (Bn, half)),
                 jnp.broadcast_to(jnp.sum(prod[:, SSD_N:2 * SSD_N], axis=-1, keepdims=True), (Bn, half))], axis=1)
            acc = acc + gx * jnp.exp(csx[rt] - csx[rs]) * dtx[rs] * xs[rs]
        ypart_ref[rt, :] = acc


def _s_mix_a(x, w, st, *, L, Bn):
    R, D = x.shape
    W = w["lcw"].shape[1]
    CD = w["scw"].shape[1]
    SI = HEADS * HEAD_P
    sds = jax.ShapeDtypeStruct
    return pl.pallas_call(
        functools.partial(_s_mix_a_kernel, L=L, Bn=Bn),
        out_shape=[sds((R, W), f32), sds((R, SI), f32), sds((R, CD - SI), f32), sds((R, SI), f32),
                   sds((R, SI), f32), sds((R, SI), f32), sds((Bn, LANES), f32),
                   sds((3 * Bn, W), f32), sds((Bn, W), f32), sds((3 * Bn, CD), f32)],
        compiler_params=pltpu.CompilerParams(vmem_limit_bytes=VMEM_LIMIT),
        name="s_mix_a",
    )(x, w["gmix"], w["win"], w["lcw"], w["lcb"], w["wg"], w["bg"], w["ap"], w["lnorm"],
      w["scw"], w["scb"], w["dtb"], w["alog"], w["dvec"], st["lc"], st["lh"], st["sc"])


def _s_ssd_kernel(dec_ref, xw_ref, bc_ref, st_ref, yraw_ref, nst_ref, *, bb):
    i = pl.program_id(0)
    GW = (HEADS // 2) * HEAD_P
    for j in range(bb):
        xw = xw_ref[j].astype(bf16)
        bc = bc_ref[j].astype(bf16)
        ys = []
        for g in range(2):
            sg = st_ref[j, g * GW:(g + 1) * GW, :]
            upd = lax.dot_general(xw[:, g * GW:(g + 1) * GW], bc[:, g * SSD_N:(g + 1) * SSD_N],
                                  (((0,), (0,)), ((), ())), preferred_element_type=f32)
            cg = bc[:, 2 * SSD_N + g * SSD_N:2 * SSD_N + (g + 1) * SSD_N]
            ys.append(lax.dot_general(cg, sg.astype(bf16), (((1,), (1,)), ((), ())),
                                      preferred_element_type=f32))
            for hh in range(HEADS // 2):
                h = g * (HEADS // 2) + hh
                r0 = g * GW + hh * HEAD_P
                nst_ref[j, r0:r0 + HEAD_P, :] = (dec_ref[(i * bb + j) * HEADS + h] * sg[hh * HEAD_P:(hh + 1) * HEAD_P, :]
                                                 + upd[hh * HEAD_P:(hh + 1) * HEAD_P, :])
        yraw_ref[j] = jnp.concatenate(ys, axis=1)


def _s_ssd(dec, xw, bc, st, *, bb):
    Bn, Lp, SI = xw.shape
    blk3 = lambda c: pl.BlockSpec((bb, Lp, c), lambda i: (i, 0, 0))
    st_spec = pl.BlockSpec((bb, SI, SSD_N), lambda i: (i, 0, 0))
    return pl.pallas_call(
        functools.partial(_s_ssd_kernel, bb=bb),
        grid=(Bn // bb,),
        in_specs=[pl.BlockSpec(memory_space=pltpu.SMEM), blk3(SI), blk3(bc.shape[2]), st_spec],
        out_specs=[blk3(SI), st_spec],
        out_shape=[jax.ShapeDtypeStruct((Bn, Lp, SI), f32), jax.ShapeDtypeStruct((Bn, SI, SSD_N), f32)],
        compiler_params=pltpu.CompilerParams(dimension_semantics=("arbitrary",), vmem_limit_bytes=VMEM_LIMIT),
        name="s_ssd",
    )(dec, xw, bc, st)


def _s_mix_c_kernel(x_ref, lru_ref, ypart_ref, yraw_ref, ecs_ref, z_ref, snorm_ref, wout_ref, o_ref):
    W = lru_ref.shape[1]
    SI = ypart_ref.shape[1]
    ys = ypart_ref[...] + yraw_ref[...] * ecs_ref[...]
    ssd_out = _rms(ys * _silu(z_ref[...]), snorm_ref[...]).astype(bf16)
    o_ref[...] = (x_ref[...] + _dot(lru_ref[...].astype(bf16), wout_ref[0:W, :])
                  + _dot(ssd_out, wout_ref[W:W + SI, :]))


def _s_mix_c(x, lru, ypart, yraw, ecs, z, w):
    return pl.pallas_call(
        _s_mix_c_kernel,
        out_shape=jax.ShapeDtypeStruct(x.shape, f32),
        compiler_params=pltpu.CompilerParams(vmem_limit_bytes=VMEM_LIMIT),
        name="s_mix_c",
    )(x, lru, ypart, yraw, ecs, z, w["snorm"], w["wout"])


def _s_ffn_kernel(x_ref, g_ref, wug_ref, wuv_ref, cwg_ref, cwv_ref, cbg_ref, cbv_ref, wdn_ref, gfin_ref,
                  stg_ref, stv_ref, o_ref, nstg_ref, nstv_ref, hn_sc, acc_sc, *, L, Bn, final):
    c = pl.program_id(0)

    @pl.when(c == 0)
    def _():
        hn_sc[...] = _rms(x_ref[...], g_ref[...]).astype(bf16)
        acc_sc[...] = jnp.zeros(acc_sc.shape, f32)

    hn = hn_sc[...]
    yg, nstg = _tm_conv(stg_ref[...], _dot(hn, wug_ref[...]), cwg_ref[...], cbg_ref[...], Bn, L)
    yv, nstv = _tm_conv(stv_ref[...], _dot(hn, wuv_ref[...]), cwv_ref[...], cbv_ref[...], Bn, L)
    nstg_ref[...] = nstg
    nstv_ref[...] = nstv
    acc_sc[...] += _dot((jax.nn.gelu(yg) * yv).astype(bf16), wdn_ref[...])

    @pl.when(c == pl.num_programs(0) - 1)
    def _():
        out = x_ref[...] + acc_sc[...]
        if final:
            out = _rms(out, gfin_ref[...])
        o_ref[...] = out


def _s_ffn(x, w, st_fc, gfin, *, L, Bn, FC, final):
    R, D = x.shape
    F = w["wdn"].shape[0]
    NC = F // FC
    full = lambda a: pl.BlockSpec(a.shape, lambda c: (0,) * a.ndim)
    colg = lambda rows: pl.BlockSpec((rows, FC), lambda c: (0, c))
    colv = lambda rows: pl.BlockSpec((rows, FC), lambda c: (0, c + NC))
    return pl.pallas_call(
        functools.partial(_s_ffn_kernel, L=L, Bn=Bn, final=final),
        grid=(NC,),
        in_specs=[full(x), full(w["gffn"]), colg(D), colv(D), colg(3), colv(3), colg(1), colv(1),
                  pl.BlockSpec((FC, D), lambda c: (c, 0)), full(gfin), colg(2 * Bn), colv(2 * Bn)],
        out_specs=[full(x), colg(2 * Bn), colg(2 * Bn)],
        out_shape=[jax.ShapeDtypeStruct((R, D), f32), jax.ShapeDtypeStruct((2 * Bn, F), f32),
                   jax.ShapeDtypeStruct((2 * Bn, F), f32)],
        scratch_shapes=[pltpu.VMEM((R, D), bf16), pltpu.VMEM((R, D), f32)],
        compiler_params=pltpu.CompilerParams(dimension_semantics=("arbitrary",), vmem_limit_bytes=VMEM_LIMIT),
        name="s_ffn",
    )(x, w["gffn"], w["wup"], w["wup"], w["cw"], w["cw"], w["cb"], w["cb"], w["wdn"], gfin, st_fc, st_fc)


def _to_tm(a):
    Bn, K, C = a.shape
    return jnp.swapaxes(a, 0, 1).reshape(K * Bn, C)


def _from_tm(a, Bn):
    K = a.shape[0] // Bn
    return jnp.swapaxes(a.reshape(K, Bn, a.shape[1]), 0, 1)


def _short_layer(x, w, st, gfin, *, L, Bn, final):
    (lru, z, bc, ypart, ecs, xw, dec, n_lc, n_lh, n_sc) = _s_mix_a(x, w, st, L=L, Bn=Bn)
    Lp = -(-L // SUBLANES) * SUBLANES
    bm = lambda a: jnp.pad(_from_tm(a, Bn), ((0, 0), (0, Lp - L), (0, 0)))
    yraw, n_ss = _s_ssd(dec[:, :HEADS].reshape(-1), bm(xw), bm(bc), st["ss"], bb=SUBLANES)
    x = _s_mix_c(x, lru, ypart, _to_tm(yraw[:, :L]), ecs, z, w)
    x, nfg, nfv = _s_ffn(x, w, st["fc"], gfin, L=L, Bn=Bn, FC=512, final=final)
    new = dict(lc=n_lc, lh=n_lh, sc=n_sc, ss=n_ss, fc=jnp.concatenate([nfg, nfv], axis=1))
    return x, new


def _prep_layer(l, norm_mix, w_in, lru_conv_w, lru_conv_b, lru_wa, lru_ba, lru_wx, lru_bx, lru_a_param,
                lru_out_norm, ssd_conv_w, ssd_conv_b, ssd_dt_bias, ssd_a_log, ssd_d, ssd_out_norm, w_out,
                norm_ffn, ffn_w_up, ffn_conv_w, ffn_conv_b, ffn_w_down):
    W = lru_conv_w.shape[2]
    CD = ssd_conv_w.shape[2]
    SI = HEADS * HEAD_P
    wl = w_in[l]
    o1, o2, o3, o4 = W, 2 * W, 2 * W + SI, 2 * W + SI + CD
    win = jnp.concatenate([wl[:, o3:o4], wl[:, 0:o3], jnp.pad(wl[:, o4:], ((0, 0), (0, LANES - HEADS)))],
                          axis=1).astype(bf16)
    hd = W // HEADS
    per = 256 // hd

    def blockdiag(wh):
        eye = jnp.eye(per, dtype=wh.dtype)
        return (eye[:, None, :, None] * wh[:, :, None, :]).reshape(per * hd, per * hd)

    wg = jnp.stack([jnp.concatenate([blockdiag(lru_wa[l, j * per:(j + 1) * per]),
                                     blockdiag(lru_wx[l, j * per:(j + 1) * per])], axis=1)
                    for j in range(HEADS // per)]).astype(bf16)
    row = lambda v: v.reshape(1, -1).astype(f32)
    pad_h = lambda v: jnp.pad(v.astype(f32), (0, LANES - HEADS)).reshape(1, LANES)
    return dict(
        gmix=row(norm_mix[l]), win=win, lcw=lru_conv_w[l], lcb=row(lru_conv_b[l]), wg=wg,
        bg=jnp.stack([lru_ba[l], lru_bx[l]]), ap=row(lru_a_param[l]), lnorm=row(lru_out_norm[l]),
        scw=ssd_conv_w[l], scb=row(ssd_conv_b[l]), dtb=pad_h(ssd_dt_bias[l]), alog=pad_h(ssd_a_log[l]),
        dvec=row(jnp.repeat(ssd_d[l], HEAD_P)), snorm=row(ssd_out_norm[l]), wout=w_out[l].astype(bf16),
        gffn=row(norm_ffn[l]), wup=ffn_w_up[l].astype(bf16), cw=ffn_conv_w[l], cb=row(ffn_conv_b[l]),
        wdn=ffn_w_down[l].astype(bf16))


def _tile(S, cap):
    T = min(S, cap)
    while S % T:
        T -= CHUNK
    return T


def kernel(x_prompt, x_sample, state_lru_conv, state_lru_h, state_ssd_conv, state_ssd, state_ffn_conv, meta_tokens, norm_mix, w_in, lru_conv_w, lru_conv_b, lru_wa, lru_ba, lru_wx, lru_bx, lru_a_param, lru_out_norm, ssd_conv_w, ssd_conv_b, ssd_dt_bias, ssd_a_log, ssd_d, ssd_out_norm, w_out, norm_ffn, ffn_w_up, ffn_conv_w, ffn_conv_b, ffn_w_down, norm_final):
    B, S, D = x_prompt.shape
    Bs, Ls, _ = x_sample.shape
    depth = w_in.shape[0]
    n_meta = meta_tokens.shape[0]
    W = lru_conv_w.shape[2]
    CD = ssd_conv_w.shape[2]
    SI = HEADS * HEAD_P
    F2 = ffn_w_up.shape[2]
    ws = [_prep_layer(l, norm_mix, w_in, lru_conv_w, lru_conv_b, lru_wa, lru_ba, lru_wx, lru_bx, lru_a_param,
                      lru_out_norm, ssd_conv_w, ssd_conv_b, ssd_dt_bias, ssd_a_log, ssd_d, ssd_out_norm, w_out,
                      norm_ffn, ffn_w_up, ffn_conv_w, ffn_conv_b, ffn_w_down) for l in range(depth)]
    gfin = norm_final.reshape(1, D).astype(f32)

    xm = jnp.broadcast_to(meta_tokens.astype(f32)[:, None, :], (n_meta, B, D)).reshape(n_meta * B, D)
    zero = dict(lc=jnp.zeros((3 * B, W), f32), lh=jnp.zeros((B, W), f32), sc=jnp.zeros((3 * B, CD), f32),
                ss=jnp.zeros((B, SI, SSD_N), f32), fc=jnp.zeros((2 * B, F2), f32))
    meta_st = []
    for l in range(depth):
        xm, new = _short_layer(xm, ws[l], zero, gfin, L=n_meta, Bn=B, final=False)
        meta_st.append(new)

    Tm = _tile(S, 256)
    Tf = _tile(S, 512)
    xp = x_prompt
    p_lc, p_lh, p_sc, p_ss, p_fc = [], [], [], [], []
    for l in range(depth):
        m = meta_st[l]
        st = dict(lc=_from_tm(m["lc"], B), lh=m["lh"].reshape(B, 1, W), sc=_from_tm(m["sc"], B), ss=m["ss"])
        xp, lc, lh, sc, ss = _p_mixer(xp, ws[l], st, T=Tm)
        xp, fc = _p_ffn(xp, ws[l], _from_tm(m["fc"], B), gfin, T=Tf, FC=512, final=(l == depth - 1))
        p_lc.append(lc); p_lh.append(lh.reshape(B, W)); p_sc.append(sc)
        p_ss.append(ss.reshape(B, HEADS, HEAD_P, SSD_N)); p_fc.append(fc)

    xs = _to_tm(x_sample)
    s_lc, s_lh, s_sc, s_ss, s_fc = [], [], [], [], []
    for l in range(depth):
        st = dict(lc=_to_tm(state_lru_conv[l]), lh=state_lru_h[l], sc=_to_tm(state_ssd_conv[l]),
                  ss=state_ssd[l].reshape(Bs, SI, SSD_N), fc=_to_tm(state_ffn_conv[l]))
        xs, new = _short_layer(xs, ws[l], st, gfin, L=Ls, Bn=Bs, final=(l == depth - 1))
        s_lc.append(_from_tm(new["lc"], Bs)); s_lh.append(new["lh"]); s_sc.append(_from_tm(new["sc"], Bs))
        s_ss.append(new["ss"].reshape(Bs, HEADS, HEAD_P, SSD_N)); s_fc.append(_from_tm(new["fc"], Bs))

    st = jnp.stack
    return (xp, _from_tm(xs, Bs), st(p_lc), st(p_lh), st(p_sc), st(p_ss), st(p_fc),
            st(s_lc), st(s_lh), st(s_sc), st(s_ss), st(s_fc))
```

```python
import functools

import jax
import jax.numpy as jnp
from jax import lax
from jax.experimental import pallas as pl
from jax.experimental.pallas import tpu as pltpu

f32 = jnp.float32
bf16 = jnp.bfloat16

EPS = 1e-6
LRU_C = 8.0
HEADS = 8
HEAD_P = 64
SSD_N = 128
CHUNK = 128
LANES = 128
SUBLANES = 8
VMEM_LIMIT = 56 * 1024 * 1024
HI = lax.Precision.HIGHEST


def _rms(x, g):
    return x * lax.rsqrt(jnp.mean(x * x, axis=-1, keepdims=True) + EPS) * g


def _silu(x):
    return x * jax.nn.sigmoid(x)


def _softplus(x):
    return jnp.maximum(x, 0.0) + jnp.log1p(jnp.exp(-jnp.abs(x)))


def _dot(a, b):
    return jnp.dot(a, b, preferred_element_type=f32)


def _lru_coeffs(xc, pre_r, pre_i, a_param):
    r = jax.nn.sigmoid(pre_r)
    i = jax.nn.sigmoid(pre_i)
    log_a = -LRU_C * r * _softplus(-a_param)
    a = jnp.exp(log_a)
    th = jnp.tanh(log_a)
    mult = jnp.sqrt(-2.0 * th / (1.0 - th))
    return a, mult * (i * xc)


def _gates(xc, wg_ref, bg_ref):
    half = wg_ref.shape[1]
    pr, pi = [], []
    for j in range(wg_ref.shape[0]):
        ri = _dot(xc[:, j * half:(j + 1) * half].astype(bf16), wg_ref[j])
        pr.append(ri[:, :half])
        pi.append(ri[:, half:])
    bg = bg_ref[...]
    return jnp.concatenate(pr, axis=1) + bg[0:1], jnp.concatenate(pi, axis=1) + bg[1:2]


def _head_expand():
    r = lax.broadcasted_iota(jnp.int32, (LANES, HEADS * HEAD_P), 0)
    c = lax.broadcasted_iota(jnp.int32, (LANES, HEADS * HEAD_P), 1)
    return (r == lax.shift_right_logical(c, 6)).astype(f32)


def _scan_rows(a, b, h0):
    T, C = a.shape
    G = T // SUBLANES
    a3 = a.reshape(G, SUBLANES, C)
    b3 = b.reshape(G, SUBLANES, C)
    sub = lax.broadcasted_iota(jnp.int32, a3.shape, 1)
    for d in (1, 2, 4):
        ar = pltpu.roll(a3, d, axis=1)
        br = pltpu.roll(b3, d, axis=1)
        m = sub >= d
        b3 = jnp.where(m, a3 * br + b3, b3)
        a3 = jnp.where(m, a3 * ar, a3)
    hs = []
    hp = h0
    for gi in range(G):
        hb = a3[gi] * hp + b3[gi]
        hs.append(hb)
        hp = hb[SUBLANES - 1:SUBLANES, :]
    return jnp.concatenate(hs, axis=0), hp


def _p_mixer_kernel(x_ref, gmix_ref, win_ref, lcw_ref, lcb_ref, wg_ref, bg_ref, ap_ref, lnorm_ref,
                    scw_ref, scb_ref, dtb_ref, alog_ref, dvec_ref, snorm_ref, wout_ref,
                    st_lc_ref, st_lh_ref, st_sc_ref, st_ss_ref,
                    o_ref, n_lc_ref, n_lh_ref, n_sc_ref, n_ss_ref,
                    ext_l, ext_s, hcar, s_pair, ys_sc, xb_sc, dt_sc, *, T):
    t = pl.program_id(1)
    nt = pl.num_programs(1)
    W = ext_l.shape[1]
    CD = ext_s.shape[1]
    SI = HEADS * HEAD_P
    NP = HEADS // 2

    @pl.when(t == 0)
    def _():
        ext_l[5:8, :] = st_lc_ref[0]
        ext_s[5:8, :] = st_sc_ref[0]
        hcar[0:1, :] = st_lh_ref[0]
        for j in range(NP):
            s_pair[j] = st_ss_ref[0, j * LANES:(j + 1) * LANES, :].T

    x = x_ref[...].reshape(T, x_ref.shape[-1])
    hn = _rms(x, gmix_ref[...]).astype(bf16)

    ext_l[8:T + 8, :] = _dot(hn, win_ref[:, CD:CD + W])
    lcw = lcw_ref[...]
    xc = lcb_ref[...] + lcw[3:4] * ext_l[8:T + 8, :]
    for k in range(3):
        xc = xc + lcw[k:k + 1] * ext_l[5 + k:5 + k + T, :]
    ext_l[5:8, :] = ext_l[T + 5:T + 8, :]
    pre_r, pre_i = _gates(xc, wg_ref, bg_ref)
    a, bt = _lru_coeffs(xc, pre_r, pre_i, ap_ref[...])
    hl, hlast = _scan_rows(a, bt, hcar[0:1, :])
    hcar[0:1, :] = hlast
    gate = _dot(hn, win_ref[:, CD + W:CD + 2 * W])
    lru_out = _rms(hl * jax.nn.gelu(gate), lnorm_ref[...]).astype(bf16)

    ext_s[8:T + 8, :] = _dot(hn, win_ref[:, 0:CD])
    scw = scw_ref[...]
    xb = scb_ref[...] + scw[3:4] * ext_s[8:T + 8, :]
    for k in range(3):
        xb = xb + scw[k:k + 1] * ext_s[5 + k:5 + k + T, :]
    ext_s[5:8, :] = ext_s[T + 5:T + 8, :]
    xb_sc[...] = _silu(xb)
    dt_sc[...] = _softplus(_dot(hn, win_ref[:, CD + 3 * W:CD + 3 * W + LANES]) + dtb_ref[...])

    a_row = -jnp.exp(alog_ref[...])
    row = lax.broadcasted_iota(jnp.int32, (CHUNK, CHUNK), 0)
    col = lax.broadcasted_iota(jnp.int32, (CHUNK, CHUNK), 1)
    causal = col <= row
    ltri = causal.astype(f32)
    lane_l = col < HEAD_P
    eexp = _head_expand()
    dvec = dvec_ref[...]
    for q in range(T // CHUNK):
        r0 = q * CHUNK
        xs_c = xb_sc[r0:r0 + CHUNK, 0:SI]
        dt_c = dt_sc[r0:r0 + CHUNK, :]
        cs_col = jnp.dot(ltri, dt_c * a_row, precision=HI, preferred_element_type=f32)
        cs_row = cs_col.T[0:HEADS]
        dt_row = dt_c.T[0:HEADS]
        ecol = jnp.exp(cs_col)
        w8 = jnp.exp(cs_row[:, CHUNK - 1:CHUNK] - cs_row) * dt_row
        dec_x = jnp.exp(jnp.dot(cs_col[CHUNK - SUBLANES:CHUNK], eexp, precision=HI,
                                preferred_element_type=f32)[SUBLANES - 1:SUBLANES])
        for g in range(2):
            bm_t = xb_sc[r0:r0 + CHUNK, SI + g * SSD_N:SI + (g + 1) * SSD_N].T
            cm = xb_sc[r0:r0 + CHUNK, SI + 2 * SSD_N + g * SSD_N:SI + 2 * SSD_N + (g + 1) * SSD_N]
            scores = _dot(cm.astype(bf16), bm_t.astype(bf16))
            for jj in range(2):
                j = 2 * g + jj
                lhs, lhs2 = [], []
                for h in (2 * j, 2 * j + 1):
                    lmat = jnp.where(causal, jnp.exp(cs_col[:, h:h + 1] - cs_row[h:h + 1]), 0.0)
                    lhs.append(scores * lmat * dt_row[h:h + 1])
                    lhs.append(cm * ecol[:, h:h + 1])
                    lhs2.append(bm_t * w8[h:h + 1])
                xp = xs_c[:, j * LANES:(j + 1) * LANES]
                sp = s_pair[j]
                xl = jnp.where(lane_l, xp, 0.0)
                xr = jnp.where(lane_l, 0.0, xp)
                rhs = jnp.concatenate([xl, jnp.where(lane_l, sp, 0.0), xr, jnp.where(lane_l, 0.0, sp)], axis=0)
                y = _dot(jnp.concatenate(lhs, axis=1).astype(bf16), rhs.astype(bf16))
                upd = _dot(jnp.concatenate(lhs2, axis=1).astype(bf16),
                           jnp.concatenate([xl, xr], axis=0).astype(bf16))
                s_pair[j] = dec_x[:, j * LANES:(j + 1) * LANES] * sp + upd
                ys_sc[r0:r0 + CHUNK, j * LANES:(j + 1) * LANES] = y + xp * dvec[:, j * LANES:(j + 1) * LANES]

    z = _dot(hn, win_ref[:, CD + 2 * W:CD + 3 * W])
    ssd_out = _rms(ys_sc[...] * _silu(z), snorm_ref[...]).astype(bf16)
    o_ref[...] = x + _dot(lru_out, wout_ref[0:W, :]) + _dot(ssd_out, wout_ref[W:W + SI, :])

    @pl.when(t == nt - 1)
    def _():
        n_lc_ref[0] = ext_l[5:8, :]
        n_sc_ref[0] = ext_s[5:8, :]
        n_lh_ref[0] = hcar[0:1, :]
        for j in range(NP):
            n_ss_ref[0, j * LANES:(j + 1) * LANES, :] = s_pair[j].T


def _const_spec(shape):
    nd = len(shape)
    return pl.BlockSpec(shape, lambda *_: (0,) * nd, pipeline_mode=pl.Buffered(1))


def _p_mixer(x, w, st, *, T, B):
    if x.ndim == 3:
        _, S, D = x.shape
        in_spec = pl.BlockSpec((1, T, D), lambda b, t: (b, t, 0))
    else:
        S, D = x.shape[0], x.shape[1] // B
        in_spec = pl.BlockSpec((T, D), lambda b, t: (t, b))
    W = w["lcw"].shape[1]
    CD = w["scw"].shape[1]
    SI = HEADS * HEAD_P
    consts = [w["gmix"], w["win"], w["lcw"], w["lcb"], w["wg"], w["bg"], w["ap"], w["lnorm"],
              w["scw"], w["scb"], w["dtb"], w["alog"], w["dvec"], w["snorm"], w["wout"]]
    st_specs = [pl.BlockSpec((1, 3, W), lambda b, t: (b, 0, 0)),
                pl.BlockSpec((1, 1, W), lambda b, t: (b, 0, 0)),
                pl.BlockSpec((1, 3, CD), lambda b, t: (b, 0, 0)),
                pl.BlockSpec((1, SI, SSD_N), lambda b, t: (b, 0, 0))]
    return pl.pallas_call(
        functools.partial(_p_mixer_kernel, T=T),
        grid=(B, S // T),
        in_specs=[in_spec] + [_const_spec(c.shape) for c in consts] + st_specs,
        out_specs=[pl.BlockSpec((T, D), lambda b, t: (t, b))] + st_specs,
        out_shape=[jax.ShapeDtypeStruct((S, B * D), f32),
                   jax.ShapeDtypeStruct((B, 3, W), f32), jax.ShapeDtypeStruct((B, 1, W), f32),
                   jax.ShapeDtypeStruct((B, 3, CD), f32), jax.ShapeDtypeStruct((B, SI, SSD_N), f32)],
        scratch_shapes=[pltpu.VMEM((T + 8, W), f32), pltpu.VMEM((T + 8, CD), f32),
                        pltpu.VMEM((SUBLANES, W), f32), pltpu.VMEM((HEADS // 2, SSD_N, LANES), f32),
                        pltpu.VMEM((T, SI), f32), pltpu.VMEM((T, CD), f32), pltpu.VMEM((T, LANES), f32)],
        compiler_params=pltpu.CompilerParams(dimension_semantics=("arbitrary", "arbitrary"),
                                             vmem_limit_bytes=VMEM_LIMIT),
        name="p_mixer",
    )(x, *consts, st["lc"], st["lh"], st["sc"], st["ss"])


def _p_ffn_kernel(x_ref, g_ref, wup_ref, cw_ref, cb_ref, wdn_ref, gfin_ref, st_ref,
                  o_ref, nst_ref, carry, *, TR, Bn, FC, final):
    i = pl.program_id(0)
    F = wdn_ref.shape[0]

    @pl.when(i == 0)
    def _():
        carry[...] = st_ref[...]

    x = x_ref[...]
    hn = _rms(x, g_ref[...]).astype(bf16)
    acc = jnp.zeros(x.shape, f32)
    for c in range(F // FC):
        ys = []
        for half in range(2):
            c0 = half * F + c * FC
            u = _dot(hn, wup_ref[:, c0:c0 + FC])
            ext = jnp.concatenate([carry[:, c0:c0 + FC], u], axis=0)
            cw = cw_ref[:, c0:c0 + FC]
            ys.append(cb_ref[:, c0:c0 + FC] + cw[0:1] * ext[0:TR] + cw[1:2] * ext[Bn:Bn + TR] + cw[2:3] * u)
            carry[:, c0:c0 + FC] = ext[TR:TR + 2 * Bn]
        act = (jax.nn.gelu(ys[0]) * ys[1]).astype(bf16)
        acc = acc + _dot(act, wdn_ref[c * FC:(c + 1) * FC, :])
    out = x + acc
    if final:
        out = _rms(out, gfin_ref[...])
    o_ref[...] = out

    @pl.when(i == pl.num_programs(0) - 1)
    def _():
        nst_ref[...] = carry[...]


def _p_ffn(x, w, st_fc, gfin, *, TR, Bn, FC, final):
    R, D = x.shape
    F2 = w["wup"].shape[1]
    consts = [w["gffn"], w["wup"], w["cw"], w["cb"], w["wdn"], gfin, st_fc]
    x_spec = pl.BlockSpec((TR, D), lambda i: (i, 0))
    return pl.pallas_call(
        functools.partial(_p_ffn_kernel, TR=TR, Bn=Bn, FC=FC, final=final),
        grid=(R // TR,),
        in_specs=[x_spec] + [_const_spec(c.shape) for c in consts],
        out_specs=[x_spec, pl.BlockSpec((2 * Bn, F2), lambda i: (0, 0))],
        out_shape=[jax.ShapeDtypeStruct((R, D), f32), jax.ShapeDtypeStruct((2 * Bn, F2), f32)],
        scratch_shapes=[pltpu.VMEM((2 * Bn, F2), f32)],
        compiler_params=pltpu.CompilerParams(dimension_semantics=("arbitrary",), vmem_limit_bytes=VMEM_LIMIT),
        name="p_ffn",
    )(x, *consts)


def _tm_conv(state, u, w, b, n_rows, L):
    K = w.shape[0]
    ext = jnp.concatenate([state, u], axis=0)
    R = L * n_rows
    y = b
    for k in range(K):
        y = y + w[k:k + 1] * ext[k * n_rows:k * n_rows + R]
    return y, ext[R:R + (K - 1) * n_rows]


def _s_mix_a_kernel(x_ref, gmix_ref, win_ref, lcw_ref, lcb_ref, wg_ref, bg_ref, ap_ref, lnorm_ref,
                    scw_ref, scb_ref, dtb_ref, alog_ref, dvec_ref,
                    st_lc_ref, st_lh_ref, st_sc_ref,
                    lru_ref, z_ref, bc_ref, ypart_ref, ecs_ref, xw_ref, dec_ref,
                    n_lc_ref, n_lh_ref, n_sc_ref, *, L, Bn):
    W = lcw_ref.shape[1]
    CD = scw_ref.shape[1]
    SI = HEADS * HEAD_P
    hn = _rms(x_ref[...], gmix_ref[...]).astype(bf16)

    xc, n_lc = _tm_conv(st_lc_ref[...], _dot(hn, win_ref[:, CD:CD + W]), lcw_ref[...], lcb_ref[...], Bn, L)
    n_lc_ref[...] = n_lc
    pre_r, pre_i = _gates(xc, wg_ref, bg_ref)
    a, bt = _lru_coeffs(xc, pre_r, pre_i, ap_ref[...])
    hp = st_lh_ref[...]
    hs = []
    for t in range(L):
        hp = a[t * Bn:(t + 1) * Bn] * hp + bt[t * Bn:(t + 1) * Bn]
        hs.append(hp)
    n_lh_ref[...] = hp
    gate = _dot(hn, win_ref[:, CD + W:CD + 2 * W])
    lru_ref[...] = _rms(jnp.concatenate(hs, axis=0) * jax.nn.gelu(gate), lnorm_ref[...])
    z_ref[...] = _dot(hn, win_ref[:, CD + 2 * W:CD + 3 * W])

    xb, n_sc = _tm_conv(st_sc_ref[...], _dot(hn, win_ref[:, 0:CD]), scw_ref[...], scb_ref[...], Bn, L)
    n_sc_ref[...] = n_sc
    xb = _silu(xb)
    xs = xb[:, 0:SI]
    bc_ref[...] = xb[:, SI:CD]
    dt = _softplus(_dot(hn, win_ref[:, CD + 3 * W:CD + 3 * W + LANES]) + dtb_ref[...])
    dA = dt * (-jnp.exp(alog_ref[...]))
    css = []
    cs = jnp.zeros((Bn, LANES), f32)
    for t in range(L):
        cs = cs + dA[t * Bn:(t + 1) * Bn]
        css.append(cs)
    dec_ref[...] = jnp.exp(cs)
    eexp = _head_expand()
    csx = jnp.dot(jnp.concatenate(css, axis=0), eexp, precision=HI, preferred_element_type=f32)
    dtx = jnp.dot(dt, eexp, precision=HI, preferred_element_type=f32)
    ecs_ref[...] = jnp.exp(csx)
    cs_end = csx[(L - 1) * Bn:L * Bn]
    dvec = dvec_ref[...]
    half = SI // 2
    for t in range(L):
        rt = slice(t * Bn, (t + 1) * Bn)
        xw_ref[rt, :] = xs[rt] * jnp.exp(cs_end - csx[rt]) * dtx[rt]
        acc = dvec * xs[rt]
        cm = xb[rt, SI + 2 * SSD_N:CD]
        for s in range(t + 1):
            rs = slice(s * Bn, (s + 1) * Bn)
            prod = cm * xb[rs, SI:SI + 2 * SSD_N]
            gx = jnp.concatenate(
                [jnp.broadcast_to(jnp.sum(prod[:, 0:SSD_N], axis=-1, keepdims=True), (Bn, half)),
                 jnp.broadcast_to(jnp.sum(prod[:, SSD_N:2 * SSD_N], axis=-1, keepdims=True), (Bn, half))], axis=1)
            acc = acc + gx * jnp.exp(csx[rt] - csx[rs]) * dtx[rs] * xs[rs]
        ypart_ref[rt, :] = acc


def _s_mix_a(x, w, st, *, L, Bn):
    R, D = x.shape
    W = w["lcw"].shape[1]
    CD = w["scw"].shape[1]
    SI = HEADS * HEAD_P
    sds = jax.ShapeDtypeStruct
    return pl.pallas_call(
        functools.partial(_s_mix_a_kernel, L=L, Bn=Bn),
        out_shape=[sds((R, W), f32), sds((R, SI), f32), sds((R, CD - SI), f32), sds((R, SI), f32),
                   sds((R, SI), f32), sds((R, SI), f32), sds((Bn, LANES), f32),
                   sds((3 * Bn, W), f32), sds((Bn, W), f32), sds((3 * Bn, CD), f32)],
        compiler_params=pltpu.CompilerParams(vmem_limit_bytes=VMEM_LIMIT),
        name="s_mix_a",
    )(x, w["gmix"], w["win"], w["lcw"], w["lcb"], w["wg"], w["bg"], w["ap"], w["lnorm"],
      w["scw"], w["scb"], w["dtb"], w["alog"], w["dvec"], st["lc"], st["lh"], st["sc"])


def _s_ssd_kernel(dec_ref, xw_ref, bc_ref, st_ref, yraw_ref, nst_ref, *, bb):
    i = pl.program_id(0)
    GW = (HEADS // 2) * HEAD_P
    for j in range(bb):
        xw = xw_ref[j].astype(bf16)
        bc = bc_ref[j].astype(bf16)
        ys = []
        for g in range(2):
            sg = st_ref[j, g * GW:(g + 1) * GW, :]
            upd = lax.dot_general(xw[:, g * GW:(g + 1) * GW], bc[:, g * SSD_N:(g + 1) * SSD_N],
                                  (((0,), (0,)), ((), ())), preferred_element_type=f32)
            cg = bc[:, 2 * SSD_N + g * SSD_N:2 * SSD_N + (g + 1) * SSD_N]
            ys.append(lax.dot_general(cg, sg.astype(bf16), (((1,), (1,)), ((), ())),
                                      preferred_element_type=f32))
            for hh in range(HEADS // 2):
                h = g * (HEADS // 2) + hh
                r0 = g * GW + hh * HEAD_P
                nst_ref[j, r0:r0 + HEAD_P, :] = (dec_ref[(i * bb + j) * HEADS + h] * sg[hh * HEAD_P:(hh + 1) * HEAD_P, :]
                                                 + upd[hh * HEAD_P:(hh + 1) * HEAD_P, :])
        yraw_ref[j] = jnp.concatenate(ys, axis=1)


def _s_ssd(dec, xw, bc, st, *, bb):
    Bn, Lp, SI = xw.shape
    blk3 = lambda c: pl.BlockSpec((bb, Lp, c), lambda i: (i, 0, 0))
    st_spec = pl.BlockSpec((bb, SI, SSD_N), lambda i: (i, 0, 0))
    return pl.pallas_call(
        functools.partial(_s_ssd_kernel, bb=bb),
        grid=(Bn // bb,),
        in_specs=[pl.BlockSpec(memory_space=pltpu.SMEM), blk3(SI), blk3(bc.shape[2]), st_spec],
        out_specs=[blk3(SI), st_spec],
        out_shape=[jax.ShapeDtypeStruct((Bn, Lp, SI), f32), jax.ShapeDtypeStruct((Bn, SI, SSD_N), f32)],
        compiler_params=pltpu.CompilerParams(dimension_semantics=("arbitrary",), vmem_limit_bytes=VMEM_LIMIT),
        name="s_ssd",
    )(dec, xw, bc, st)


def _s_mix_c_kernel(x_ref, lru_ref, ypart_ref, yraw_ref, ecs_ref, z_ref, snorm_ref, wout_ref, o_ref):
    W = lru_ref.shape[1]
    SI = ypart_ref.shape[1]
    ys = ypart_ref[...] + yraw_ref[...] * ecs_ref[...]
    ssd_out = _rms(ys * _silu(z_ref[...]), snorm_ref[...]).astype(bf16)
    o_ref[...] = (x_ref[...] + _dot(lru_ref[...].astype(bf16), wout_ref[0:W, :])
                  + _dot(ssd_out, wout_ref[W:W + SI, :]))


def _s_mix_c(x, lru, ypart, yraw, ecs, z, w):
    return pl.pallas_call(
        _s_mix_c_kernel,
        out_shape=jax.ShapeDtypeStruct(x.shape, f32),
        compiler_params=pltpu.CompilerParams(vmem_limit_bytes=VMEM_LIMIT),
        name="s_mix_c",
    )(x, lru, ypart, yraw, ecs, z, w["snorm"], w["wout"])


def _s_ffn_kernel(x_ref, g_ref, wug_ref, wuv_ref, cwg_ref, cwv_ref, cbg_ref, cbv_ref, wdn_ref, gfin_ref,
                  stg_ref, stv_ref, o_ref, nstg_ref, nstv_ref, hn_sc, acc_sc, *, L, Bn, final):
    c = pl.program_id(0)

    @pl.when(c == 0)
    def _():
        hn_sc[...] = _rms(x_ref[...], g_ref[...]).astype(bf16)
        acc_sc[...] = jnp.zeros(acc_sc.shape, f32)

    hn = hn_sc[...]
    yg, nstg = _tm_conv(stg_ref[...], _dot(hn, wug_ref[...]), cwg_ref[...], cbg_ref[...], Bn, L)
    yv, nstv = _tm_conv(stv_ref[...], _dot(hn, wuv_ref[...]), cwv_ref[...], cbv_ref[...], Bn, L)
    nstg_ref[...] = nstg
    nstv_ref[...] = nstv
    acc_sc[...] += _dot((jax.nn.gelu(yg) * yv).astype(bf16), wdn_ref[...])

    @pl.when(c == pl.num_programs(0) - 1)
    def _():
        out = x_ref[...] + acc_sc[...]
        if final:
            out = _rms(out, gfin_ref[...])
        o_ref[...] = out


def _s_ffn(x, w, st_fc, gfin, *, L, Bn, FC, final):
    R, D = x.shape
    F = w["wdn"].shape[0]
    NC = F // FC
    full = lambda a: pl.BlockSpec(a.shape, lambda c: (0,) * a.ndim)
    colg = lambda rows: pl.BlockSpec((rows, FC), lambda c: (0, c))
    colv = lambda rows: pl.BlockSpec((rows, FC), lambda c: (0, c + NC))
    return pl.pallas_call(
        functools.partial(_s_ffn_kernel, L=L, Bn=Bn, final=final),
        grid=(NC,),
        in_specs=[full(x), full(w["gffn"]), colg(D), colv(D), colg(3), colv(3), colg(1), colv(1),
                  pl.BlockSpec((FC, D), lambda c: (c, 0)), full(gfin), colg(2 * Bn), colv(2 * Bn)],
        out_specs=[full(x), colg(2 * Bn), colg(2 * Bn)],
        out_shape=[jax.ShapeDtypeStruct((R, D), f32), jax.ShapeDtypeStruct((2 * Bn, F), f32),
                   jax.ShapeDtypeStruct((2 * Bn, F), f32)],
        scratch_shapes=[pltpu.VMEM((R, D), bf16), pltpu.VMEM((R, D), f32)],
        compiler_params=pltpu.CompilerParams(dimension_semantics=("arbitrary",), vmem_limit_bytes=VMEM_LIMIT),
        name="s_ffn",
    )(x, w["gffn"], w["wup"], w["wup"], w["cw"], w["cw"], w["cb"], w["cb"], w["wdn"], gfin, st_fc, st_fc)


def _to_tm(a):
    Bn, K, C = a.shape
    return jnp.swapaxes(a, 0, 1).reshape(K * Bn, C)


def _from_tm(a, Bn):
    K = a.shape[0] // Bn
    return jnp.swapaxes(a.reshape(K, Bn, a.shape[1]), 0, 1)


def _short_layer(x, w, st, gfin, *, L, Bn, final):
    (lru, z, bc, ypart, ecs, xw, dec, n_lc, n_lh, n_sc) = _s_mix_a(x, w, st, L=L, Bn=Bn)
    Lp = -(-L // SUBLANES) * SUBLANES
    bm = lambda a: jnp.pad(_from_tm(a, Bn), ((0, 0), (0, Lp - L), (0, 0)))
    yraw, n_ss = _s_ssd(dec[:, :HEADS].reshape(-1), bm(xw), bm(bc), st["ss"], bb=SUBLANES)
    x = _s_mix_c(x, lru, ypart, _to_tm(yraw[:, :L]), ecs, z, w)
    x, nfg, nfv = _s_ffn(x, w, st["fc"], gfin, L=L, Bn=Bn, FC=512, final=final)
    new = dict(lc=n_lc, lh=n_lh, sc=n_sc, ss=n_ss, fc=jnp.concatenate([nfg, nfv], axis=1))
    return x, new


def _prep_layer(l, norm_mix, w_in, lru_conv_w, lru_conv_b, lru_wa, lru_ba, lru_wx, lru_bx, lru_a_param,
                lru_out_norm, ssd_conv_w, ssd_conv_b, ssd_dt_bias, ssd_a_log, ssd_d, ssd_out_norm, w_out,
                norm_ffn, ffn_w_up, ffn_conv_w, ffn_conv_b, ffn_w_down):
    W = lru_conv_w.shape[2]
    CD = ssd_conv_w.shape[2]
    SI = HEADS * HEAD_P
    wl = w_in[l]
    o1, o2, o3, o4 = W, 2 * W, 2 * W + SI, 2 * W + SI + CD
    win = jnp.concatenate([wl[:, o3:o4], wl[:, 0:o3], jnp.pad(wl[:, o4:], ((0, 0), (0, LANES - HEADS)))],
                          axis=1).astype(bf16)
    hd = W // HEADS
    per = 256 // hd

    def blockdiag(wh):
        eye = jnp.eye(per, dtype=wh.dtype)
        return (eye[:, None, :, None] * wh[:, :, None, :]).reshape(per * hd, per * hd)

    wg = jnp.stack([jnp.concatenate([blockdiag(lru_wa[l, j * per:(j + 1) * per]),
                                     blockdiag(lru_wx[l, j * per:(j + 1) * per])], axis=1)
                    for j in range(HEADS // per)]).astype(bf16)
    row = lambda v: v.reshape(1, -1).astype(f32)
    pad_h = lambda v: jnp.pad(v.astype(f32), (0, LANES - HEADS)).reshape(1, LANES)
    return dict(
        gmix=row(norm_mix[l]), win=win, lcw=lru_conv_w[l], lcb=row(lru_conv_b[l]), wg=wg,
        bg=jnp.stack([lru_ba[l], lru_bx[l]]), ap=row(lru_a_param[l]), lnorm=row(lru_out_norm[l]),
        scw=ssd_conv_w[l], scb=row(ssd_conv_b[l]), dtb=pad_h(ssd_dt_bias[l]), alog=pad_h(ssd_a_log[l]),
        dvec=row(jnp.repeat(ssd_d[l], HEAD_P)), snorm=row(ssd_out_norm[l]), wout=w_out[l].astype(bf16),
        gffn=row(norm_ffn[l]), wup=ffn_w_up[l].astype(bf16), cw=ffn_conv_w[l], cb=row(ffn_conv_b[l]),
        wdn=ffn_w_down[l].astype(bf16))


def _tile(S, cap):
    T = min(S, cap)
    while S % T:
        T -= CHUNK
    return T


def kernel(x_prompt, x_sample, state_lru_conv, state_lru_h, state_ssd_conv, state_ssd, state_ffn_conv, meta_tokens, norm_mix, w_in, lru_conv_w, lru_conv_b, lru_wa, lru_ba, lru_wx, lru_bx, lru_a_param, lru_out_norm, ssd_conv_w, ssd_conv_b, ssd_dt_bias, ssd_a_log, ssd_d, ssd_out_norm, w_out, norm_ffn, ffn_w_up, ffn_conv_w, ffn_conv_b, ffn_w_down, norm_final):
    B, S, D = x_prompt.shape
    Bs, Ls, _ = x_sample.shape
    depth = w_in.shape[0]
    n_meta = meta_tokens.shape[0]
    W = lru_conv_w.shape[2]
    CD = ssd_conv_w.shape[2]
    SI = HEADS * HEAD_P
    F2 = ffn_w_up.shape[2]
    ws = [_prep_layer(l, norm_mix, w_in, lru_conv_w, lru_conv_b, lru_wa, lru_ba, lru_wx, lru_bx, lru_a_param,
                      lru_out_norm, ssd_conv_w, ssd_conv_b, ssd_dt_bias, ssd_a_log, ssd_d, ssd_out_norm, w_out,
                      norm_ffn, ffn_w_up, ffn_conv_w, ffn_conv_b, ffn_w_down) for l in range(depth)]
    gfin = norm_final.reshape(1, D).astype(f32)

    xm = jnp.broadcast_to(meta_tokens.astype(f32)[:, None, :], (n_meta, B, D)).reshape(n_meta * B, D)
    zero = dict(lc=jnp.zeros((3 * B, W), f32), lh=jnp.zeros((B, W), f32), sc=jnp.zeros((3 * B, CD), f32),
                ss=jnp.zeros((B, SI, SSD_N), f32), fc=jnp.zeros((2 * B, F2), f32))
    meta_st = []
    for l in range(depth):
        xm, new = _short_layer(xm, ws[l], zero, gfin, L=n_meta, Bn=B, final=False)
        meta_st.append(new)

    Tm = _tile(S, 256)
    TRf = _tile(S * B, 512)
    xp = x_prompt
    p_lc, p_lh, p_sc, p_ss, p_fc = [], [], [], [], []
    for l in range(depth):
        m = meta_st[l]
        st = dict(lc=_from_tm(m["lc"], B), lh=m["lh"].reshape(B, 1, W), sc=_from_tm(m["sc"], B), ss=m["ss"])
        xp, lc, lh, sc, ss = _p_mixer(xp, ws[l], st, T=Tm, B=B)
        xp, fc = _p_ffn(xp.reshape(S * B, D), ws[l], m["fc"], gfin, TR=TRf, Bn=B, FC=512, final=(l == depth - 1))
        xp = xp.reshape(S, B * D)
        p_lc.append(lc); p_lh.append(lh.reshape(B, W)); p_sc.append(sc)
        p_ss.append(ss.reshape(B, HEADS, HEAD_P, SSD_N)); p_fc.append(_from_tm(fc, B))
    xp = jnp.swapaxes(xp.reshape(S, B, D), 0, 1)

    xs = _to_tm(x_sample)
    s_lc, s_lh, s_sc, s_ss, s_fc = [], [], [], [], []
    for l in range(depth):
        st = dict(lc=_to_tm(state_lru_conv[l]), lh=state_lru_h[l], sc=_to_tm(state_ssd_conv[l]),
                  ss=state_ssd[l].reshape(Bs, SI, SSD_N), fc=_to_tm(state_ffn_conv[l]))
        xs, new = _short_layer(xs, ws[l], st, gfin, L=Ls, Bn=Bs, final=(l == depth - 1))
        s_lc.append(_from_tm(new["lc"], Bs)); s_lh.append(new["lh"]); s_sc.append(_from_tm(new["sc"], Bs))
        s_ss.append(new["ss"].reshape(Bs, HEADS, HEAD_P, SSD_N)); s_fc.append(_from_tm(new["fc"], Bs))

    st = jnp.stack
    return (xp, _from_tm(xs, Bs), st(p_lc), st(p_lh), st(p_sc), st(p_ss), st(p_fc),
            st(s_lc), st(s_lh), st(s_sc), st(s_ss), st(s_fc))
```

```python
import functools

import jax
import jax.numpy as jnp
from jax import lax
from jax.experimental import pallas as pl
from jax.experimental.pallas import tpu as pltpu

f32 = jnp.float32
bf16 = jnp.bfloat16

EPS = 1e-6
LRU_C = 8.0
HEADS = 8
HEAD_P = 64
SSD_N = 128
CHUNK = 128
LANES = 128
SUBLANES = 8
VMEM_LIMIT = 56 * 1024 * 1024
HI = lax.Precision.HIGHEST


def _rms(x, g):
    return x * lax.rsqrt(jnp.mean(x * x, axis=-1, keepdims=True) + EPS) * g


def _silu(x):
    return x * jax.nn.sigmoid(x)


def _softplus(x):
    return jnp.maximum(x, 0.0) + jnp.log1p(jnp.exp(-jnp.abs(x)))


def _dot(a, b):
    return jnp.dot(a, b, preferred_element_type=f32)


def _lru_coeffs(xc, pre_r, pre_i, a_param):
    r = jax.nn.sigmoid(pre_r)
    i = jax.nn.sigmoid(pre_i)
    log_a = -LRU_C * r * _softplus(-a_param)
    a = jnp.exp(log_a)
    th = jnp.tanh(log_a)
    mult = jnp.sqrt(-2.0 * th / (1.0 - th))
    return a, mult * (i * xc)


def _gates(xc, wg_ref, bg_ref):
    half = wg_ref.shape[1]
    pr, pi = [], []
    for j in range(wg_ref.shape[0]):
        ri = _dot(xc[:, j * half:(j + 1) * half].astype(bf16), wg_ref[j])
        pr.append(ri[:, :half])
        pi.append(ri[:, half:])
    bg = bg_ref[...]
    return jnp.concatenate(pr, axis=1) + bg[0:1], jnp.concatenate(pi, axis=1) + bg[1:2]


def _head_expand():
    r = lax.broadcasted_iota(jnp.int32, (LANES, HEADS * HEAD_P), 0)
    c = lax.broadcasted_iota(jnp.int32, (LANES, HEADS * HEAD_P), 1)
    return (r == lax.shift_right_logical(c, 6)).astype(f32)


def _layer_spec(a, l):
    nd = a.ndim - 1
    return pl.BlockSpec((None,) + a.shape[1:], lambda *_: (l,) + (0,) * nd, pipeline_mode=pl.Buffered(1))


def _full_spec(a):
    nd = a.ndim
    return pl.BlockSpec(a.shape, lambda *_: (0,) * nd)


MIX_KEYS = ("gmix", "win", "lcw", "lcb", "wg", "bg", "ap", "lnorm", "scw", "scb", "dtb", "alog", "dvec")


def _scan_rows(a, b, h0):
    T, C = a.shape
    G = T // SUBLANES
    a3 = a.reshape(G, SUBLANES, C)
    b3 = b.reshape(G, SUBLANES, C)
    sub = lax.broadcasted_iota(jnp.int32, a3.shape, 1)
    for d in (1, 2, 4):
        ar = pltpu.roll(a3, d, axis=1)
        br = pltpu.roll(b3, d, axis=1)
        m = sub >= d
        b3 = jnp.where(m, a3 * br + b3, b3)
        a3 = jnp.where(m, a3 * ar, a3)
    hs = []
    hp = h0
    for gi in range(G):
        hb = a3[gi] * hp + b3[gi]
        hs.append(hb)
        hp = hb[SUBLANES - 1:SUBLANES, :]
    return jnp.concatenate(hs, axis=0), hp


def _p_mixer_kernel(x_ref, gmix_ref, win_ref, lcw_ref, lcb_ref, wg_ref, bg_ref, ap_ref, lnorm_ref,
                    scw_ref, scb_ref, dtb_ref, alog_ref, dvec_ref, snorm_ref, wout_ref,
                    st_lc_ref, st_lh_ref, st_sc_ref, st_ss_ref,
                    o_ref, n_lc_ref, n_lh_ref, n_sc_ref, n_ss_ref,
                    ext_l, ext_s, hcar, s_pair, ys_sc, xb_sc, dt_sc, gz_sc, mix_sc, hn_sc, *, T, NB):
    t = pl.program_id(1)
    nt = pl.num_programs(1)
    W = ext_l.shape[2]
    CD = ext_s.shape[2]
    SI = HEADS * HEAD_P
    NP = HEADS // 2

    @pl.when(t == 0)
    def _():
        for s in range(NB):
            ext_l[s, 5:8, :] = st_lc_ref[s]
            ext_s[s, 5:8, :] = st_sc_ref[s]
            hcar[s, 0:1, :] = st_lh_ref[s]
            for j in range(NP):
                s_pair[s, j] = st_ss_ref[s, j * LANES:(j + 1) * LANES, :].T

    a_row = -jnp.exp(alog_ref[...])
    row = lax.broadcasted_iota(jnp.int32, (CHUNK, CHUNK), 0)
    col = lax.broadcasted_iota(jnp.int32, (CHUNK, CHUNK), 1)
    causal = col <= row
    ltri = causal.astype(f32)
    lane_l = col < HEAD_P
    eexp = _head_expand()
    dvec = dvec_ref[...]
    lcw = lcw_ref[...]
    scw = scw_ref[...]

    def stage_a(s):
        def norm():
            hn_sc[s] = _rms(x_ref[s], gmix_ref[...]).astype(bf16)

        def proj(dst, d0, c0, n):
            def step():
                dst[s, d0[0]:d0[0] + T, d0[1]:d0[1] + n] = _dot(hn_sc[s], win_ref[:, c0:c0 + n])
            return step

        def dt():
            dt_sc[s] = _softplus(_dot(hn_sc[s], win_ref[:, CD + 3 * W:CD + 3 * W + LANES]) + dtb_ref[...])

        steps = [norm, proj(ext_l, (8, 0), CD, W)]
        steps += [proj(ext_s, (8, c), c, W) for c in range(0, CD, W)]
        steps += [proj(gz_sc, (0, 0), CD + W, W), proj(gz_sc, (0, W), CD + 2 * W, W), dt]
        return steps

    def stage_b(s):
        xc = lcb_ref[...] + lcw[3:4] * ext_l[s, 8:T + 8, :]
        for k in range(3):
            xc = xc + lcw[k:k + 1] * ext_l[s, 5 + k:5 + k + T, :]
        ext_l[s, 5:8, :] = ext_l[s, T + 5:T + 8, :]
        pre_r, pre_i = _gates(xc, wg_ref, bg_ref)
        a, bt = _lru_coeffs(xc, pre_r, pre_i, ap_ref[...])
        yield
        hl, hlast = _scan_rows(a, bt, hcar[s, 0:1, :])
        hcar[s, 0:1, :] = hlast
        yield
        mix_sc[s, :, 0:W] = _rms(hl * jax.nn.gelu(gz_sc[s, :, 0:W]), lnorm_ref[...]).astype(bf16)
        yield

        xb = scb_ref[...] + scw[3:4] * ext_s[s, 8:T + 8, :]
        for k in range(3):
            xb = xb + scw[k:k + 1] * ext_s[s, 5 + k:5 + k + T, :]
        ext_s[s, 5:8, :] = ext_s[s, T + 5:T + 8, :]
        xb_sc[s] = _silu(xb)
        yield

        for q in range(T // CHUNK):
            r0 = q * CHUNK
            xs_c = xb_sc[s, r0:r0 + CHUNK, 0:SI]
            dt_c = dt_sc[s, r0:r0 + CHUNK, :]
            cs_col = jnp.dot(ltri, dt_c * a_row, precision=HI, preferred_element_type=f32)
            cs_row = cs_col.T[0:HEADS]
            dt_row = dt_c.T[0:HEADS]
            ecol = jnp.exp(cs_col)
            w8 = jnp.exp(cs_row[:, CHUNK - 1:CHUNK] - cs_row) * dt_row
            dec_x = jnp.exp(jnp.dot(cs_col[CHUNK - SUBLANES:CHUNK], eexp, precision=HI,
                                    preferred_element_type=f32)[SUBLANES - 1:SUBLANES])
            for g in range(2):
                bm_t = xb_sc[s, r0:r0 + CHUNK, SI + g * SSD_N:SI + (g + 1) * SSD_N].T
                cm = xb_sc[s, r0:r0 + CHUNK, SI + 2 * SSD_N + g * SSD_N:SI + 2 * SSD_N + (g + 1) * SSD_N]
                scores = _dot(cm.astype(bf16), bm_t.astype(bf16))
                for jj in range(2):
                    j = 2 * g + jj
                    lhs, lhs2 = [], []
                    for h in (2 * j, 2 * j + 1):
                        lmat = jnp.where(causal, jnp.exp(cs_col[:, h:h + 1] - cs_row[h:h + 1]), 0.0)
                        lhs.append(scores * lmat * dt_row[h:h + 1])
                        lhs.append(cm * ecol[:, h:h + 1])
                        lhs2.append(bm_t * w8[h:h + 1])
                    xp = xs_c[:, j * LANES:(j + 1) * LANES]
                    sp = s_pair[s, j]
                    xl = jnp.where(lane_l, xp, 0.0)
                    xr = jnp.where(lane_l, 0.0, xp)
                    rhs = jnp.concatenate([xl, jnp.where(lane_l, sp, 0.0), xr, jnp.where(lane_l, 0.0, sp)], axis=0)
                    y = _dot(jnp.concatenate(lhs, axis=1).astype(bf16), rhs.astype(bf16))
                    upd = _dot(jnp.concatenate(lhs2, axis=1).astype(bf16),
                               jnp.concatenate([xl, xr], axis=0).astype(bf16))
                    s_pair[s, j] = dec_x[:, j * LANES:(j + 1) * LANES] * sp + upd
                    ys_sc[s, r0:r0 + CHUNK, j * LANES:(j + 1) * LANES] = y + xp * dvec[:, j * LANES:(j + 1) * LANES]
                yield

        mix_sc[s, :, W:W + SI] = _rms(ys_sc[s] * _silu(gz_sc[s, :, W:2 * W]), snorm_ref[...]).astype(bf16)

    def stage_c(s):
        o_ref[s] = x_ref[s] + _dot(mix_sc[s], wout_ref[...])

    for step in stage_a(0):
        step()
    for s in range(NB):
        pending = stage_a(s + 1) if s + 1 < NB else []
        for _ in stage_b(s):
            if pending:
                pending.pop(0)()
        for step in pending:
            step()
        stage_c(s)

    @pl.when(t == nt - 1)
    def _():
        for s in range(NB):
            n_lc_ref[s] = ext_l[s, 5:8, :]
            n_sc_ref[s] = ext_s[s, 5:8, :]
            n_lh_ref[s] = hcar[s, 0:1, :]
            for j in range(NP):
                n_ss_ref[s, j * LANES:(j + 1) * LANES, :] = s_pair[s, j].T


def _p_mixer(x, w, l, st, *, T, NB):
    B, S, D = x.shape
    W = w["lcw"].shape[-1]
    CD = w["scw"].shape[-1]
    SI = HEADS * HEAD_P
    consts = [w[k] for k in MIX_KEYS] + [w["snorm"], w["wout"]]
    st_specs = [pl.BlockSpec((NB, 3, W), lambda b, t: (b, 0, 0)),
                pl.BlockSpec((NB, 1, W), lambda b, t: (b, 0, 0)),
                pl.BlockSpec((NB, 3, CD), lambda b, t: (b, 0, 0)),
                pl.BlockSpec((NB, SI, SSD_N), lambda b, t: (b, 0, 0))]
    x_spec = pl.BlockSpec((NB, T, D), lambda b, t: (b, t, 0))
    return pl.pallas_call(
        functools.partial(_p_mixer_kernel, T=T, NB=NB),
        grid=(B // NB, S // T),
        in_specs=[x_spec] + [_layer_spec(c, l) for c in consts] + st_specs,
        out_specs=[x_spec] + st_specs,
        out_shape=[jax.ShapeDtypeStruct((B, S, D), f32),
                   jax.ShapeDtypeStruct((B, 3, W), f32), jax.ShapeDtypeStruct((B, 1, W), f32),
                   jax.ShapeDtypeStruct((B, 3, CD), f32), jax.ShapeDtypeStruct((B, SI, SSD_N), f32)],
        scratch_shapes=[pltpu.VMEM((NB, T + 8, W), f32), pltpu.VMEM((NB, T + 8, CD), f32),
                        pltpu.VMEM((NB, SUBLANES, W), f32), pltpu.VMEM((NB, HEADS // 2, SSD_N, LANES), f32),
                        pltpu.VMEM((NB, T, SI), f32), pltpu.VMEM((NB, T, CD), f32), pltpu.VMEM((NB, T, LANES), f32),
                        pltpu.VMEM((NB, T, 2 * W), f32), pltpu.VMEM((NB, T, W + SI), bf16),
                        pltpu.VMEM((NB, T, D), bf16)],
        compiler_params=pltpu.CompilerParams(dimension_semantics=("arbitrary", "arbitrary"),
                                             vmem_limit_bytes=VMEM_LIMIT),
        name="p_mixer",
    )(x, *consts, st["lc"], st["lh"], st["sc"], st["ss"])


def _p_ffn_kernel(x_ref, g_ref, wup_ref, cw_ref, cb_ref, wdn_ref, gfin_ref, st_ref,
                  o_ref, nst_ref, ext, carry, *, T, FC, final):
    t = pl.program_id(1)
    nt = pl.num_programs(1)
    F = wdn_ref.shape[0]

    @pl.when(t == 0)
    def _():
        carry[6:8, :] = st_ref[0]

    x = x_ref[0]
    hn = _rms(x, g_ref[...]).astype(bf16)
    acc = jnp.zeros(x.shape, f32)
    for c in range(F // FC):
        ys = []
        for half in range(2):
            c0 = half * F + c * FC
            u = _dot(hn, wup_ref[:, c0:c0 + FC])
            ext[half, 8:T + 8, :] = u
            ext[half, 6:8, :] = carry[6:8, c0:c0 + FC]
            cw = cw_ref[:, c0:c0 + FC]
            ys.append(cb_ref[:, c0:c0 + FC] + cw[0:1] * ext[half, 6:T + 6, :]
                      + cw[1:2] * ext[half, 7:T + 7, :] + cw[2:3] * u)
            carry[6:8, c0:c0 + FC] = ext[half, T + 6:T + 8, :]
        act = (jax.nn.gelu(ys[0]) * ys[1]).astype(bf16)
        acc = acc + _dot(act, wdn_ref[c * FC:(c + 1) * FC, :])
    out = x + acc
    if final:
        out = _rms(out, gfin_ref[...])
    o_ref[0] = out

    @pl.when(t == nt - 1)
    def _():
        nst_ref[0] = carry[6:8, :]


def _p_ffn(x, w, l, st_fc, gfin, *, T, FC, final):
    B, S, D = x.shape
    F2 = w["wup"].shape[-1]
    consts = [w["gffn"], w["wup"], w["cw"], w["cb"], w["wdn"]]
    st_spec = pl.BlockSpec((1, 2, F2), lambda b, t: (b, 0, 0))
    x_spec = pl.BlockSpec((1, T, D), lambda b, t: (b, t, 0))
    return pl.pallas_call(
        functools.partial(_p_ffn_kernel, T=T, FC=FC, final=final),
        grid=(B, S // T),
        in_specs=[x_spec] + [_layer_spec(c, l) for c in consts] + [_full_spec(gfin), st_spec],
        out_specs=[x_spec, st_spec],
        out_shape=[jax.ShapeDtypeStruct((B, S, D), f32), jax.ShapeDtypeStruct((B, 2, F2), f32)],
        scratch_shapes=[pltpu.VMEM((2, T + 8, FC), f32), pltpu.VMEM((SUBLANES, F2), f32)],
        compiler_params=pltpu.CompilerParams(dimension_semantics=("arbitrary", "arbitrary"),
                                             vmem_limit_bytes=VMEM_LIMIT),
        name="p_ffn",
    )(x, *consts, gfin, st_fc)


def _tm_conv(state, u, w, b, n_rows, L):
    K = w.shape[0]
    ext = jnp.concatenate([state, u], axis=0)
    R = L * n_rows
    y = b
    for k in range(K):
        y = y + w[k:k + 1] * ext[k * n_rows:k * n_rows + R]
    return y, ext[R:R + (K - 1) * n_rows]


def _s_mix_a_kernel(x_ref, gmix_ref, win_ref, lcw_ref, lcb_ref, wg_ref, bg_ref, ap_ref, lnorm_ref,
                    scw_ref, scb_ref, dtb_ref, alog_ref, dvec_ref,
                    st_lc_ref, st_lh_ref, st_sc_ref,
                    lru_ref, z_ref, bc_ref, ypart_ref, ecs_ref, xw_ref, dec_ref,
                    n_lc_ref, n_lh_ref, n_sc_ref, *, L, Bn):
    W = lcw_ref.shape[1]
    CD = scw_ref.shape[1]
    SI = HEADS * HEAD_P
    hn = _rms(x_ref[...], gmix_ref[...]).astype(bf16)

    xc, n_lc = _tm_conv(st_lc_ref[...], _dot(hn, win_ref[:, CD:CD + W]), lcw_ref[...], lcb_ref[...], Bn, L)
    n_lc_ref[...] = n_lc
    pre_r, pre_i = _gates(xc, wg_ref, bg_ref)
    a, bt = _lru_coeffs(xc, pre_r, pre_i, ap_ref[...])
    hp = st_lh_ref[...]
    hs = []
    for t in range(L):
        hp = a[t * Bn:(t + 1) * Bn] * hp + bt[t * Bn:(t + 1) * Bn]
        hs.append(hp)
    n_lh_ref[...] = hp
    gate = _dot(hn, win_ref[:, CD + W:CD + 2 * W])
    lru_ref[...] = _rms(jnp.concatenate(hs, axis=0) * jax.nn.gelu(gate), lnorm_ref[...])
    z_ref[...] = _dot(hn, win_ref[:, CD + 2 * W:CD + 3 * W])

    xb, n_sc = _tm_conv(st_sc_ref[...], _dot(hn, win_ref[:, 0:CD]), scw_ref[...], scb_ref[...], Bn, L)
    n_sc_ref[...] = n_sc
    xb = _silu(xb)
    xs = xb[:, 0:SI]
    bc_ref[...] = xb[:, SI:CD]
    dt = _softplus(_dot(hn, win_ref[:, CD + 3 * W:CD + 3 * W + LANES]) + dtb_ref[...])
    dA = dt * (-jnp.exp(alog_ref[...]))
    css = []
    cs = jnp.zeros((Bn, LANES), f32)
    for t in range(L):
        cs = cs + dA[t * Bn:(t + 1) * Bn]
        css.append(cs)
    dec_ref[...] = jnp.exp(cs)
    eexp = _head_expand()
    csx = jnp.dot(jnp.concatenate(css, axis=0), eexp, precision=HI, preferred_element_type=f32)
    dtx = jnp.dot(dt, eexp, precision=HI, preferred_element_type=f32)
    ecs_ref[...] = jnp.exp(csx)
    cs_end = csx[(L - 1) * Bn:L * Bn]
    dvec = dvec_ref[...]
    half = SI // 2
    for t in range(L):
        rt = slice(t * Bn, (t + 1) * Bn)
        xw_ref[rt, :] = xs[rt] * jnp.exp(cs_end - csx[rt]) * dtx[rt]
        acc = dvec * xs[rt]
        cm = xb[rt, SI + 2 * SSD_N:CD]
        for s in range(t + 1):
            rs = slice(s * Bn, (s + 1) * Bn)
            prod = cm * xb[rs, SI:SI + 2 * SSD_N]
            gx = jnp.concatenate(
                [jnp.broadcast_to(jnp.sum(prod[:, 0:SSD_N], axis=-1, keepdims=True), (Bn, half)),
                 jnp.broadcast_to(jnp.sum(prod[:, SSD_N:2 * SSD_N], axis=-1, keepdims=True), (Bn, half))], axis=1)
            acc = acc + gx * jnp.exp(csx[rt] - csx[rs]) * dtx[rs] * xs[rs]
        ypart_ref[rt, :] = acc


def _s_mix_a(x, w, l, st, *, L, Bn):
    R, D = x.shape
    W = w["lcw"].shape[-1]
    CD = w["scw"].shape[-1]
    SI = HEADS * HEAD_P
    sds = jax.ShapeDtypeStruct
    consts = [w[k] for k in MIX_KEYS]
    acts = [st["lc"], st["lh"], st["sc"]]
    out_shape = [sds((R, W), f32), sds((R, SI), f32), sds((R, CD - SI), f32), sds((R, SI), f32),
                 sds((R, SI), f32), sds((R, SI), f32), sds((Bn, LANES), f32),
                 sds((3 * Bn, W), f32), sds((Bn, W), f32), sds((3 * Bn, CD), f32)]
    return pl.pallas_call(
        functools.partial(_s_mix_a_kernel, L=L, Bn=Bn),
        grid=(1,),
        in_specs=[_full_spec(x)] + [_layer_spec(c, l) for c in consts] + [_full_spec(a) for a in acts],
        out_specs=[_full_spec(o) for o in out_shape],
        out_shape=out_shape,
        compiler_params=pltpu.CompilerParams(dimension_semantics=("arbitrary",), vmem_limit_bytes=VMEM_LIMIT),
        name="s_mix_a",
    )(x, *consts, *acts)


def _s_ssd_kernel(*refs, bb, aliased):
    if aliased:
        dec_ref, xw_ref, bc_ref, st_ref, _, yraw_ref, nst_ref = refs
    else:
        dec_ref, xw_ref, bc_ref, st_ref, yraw_ref, nst_ref = refs
    i = pl.program_id(0)
    GW = (HEADS // 2) * HEAD_P
    for j in range(bb):
        xw = xw_ref[j].astype(bf16)
        bc = bc_ref[j].astype(bf16)
        ys = []
        for g in range(2):
            sg = st_ref[j, g * GW:(g + 1) * GW, :]
            upd = lax.dot_general(xw[:, g * GW:(g + 1) * GW], bc[:, g * SSD_N:(g + 1) * SSD_N],
                                  (((0,), (0,)), ((), ())), preferred_element_type=f32)
            cg = bc[:, 2 * SSD_N + g * SSD_N:2 * SSD_N + (g + 1) * SSD_N]
            ys.append(lax.dot_general(cg, sg.astype(bf16), (((1,), (1,)), ((), ())),
                                      preferred_element_type=f32))
            for hh in range(HEADS // 2):
                h = g * (HEADS // 2) + hh
                r0 = g * GW + hh * HEAD_P
                nst_ref[j, r0:r0 + HEAD_P, :] = (dec_ref[(i * bb + j) * HEADS + h] * sg[hh * HEAD_P:(hh + 1) * HEAD_P, :]
                                                 + upd[hh * HEAD_P:(hh + 1) * HEAD_P, :])
        yraw_ref[j] = jnp.concatenate(ys, axis=1)


def _s_ssd(dec, xw, bc, st_all, l_in, out_all, l_out, *, bb, depth):
    Bn, Lp, SI = xw.shape
    blk3 = lambda c: pl.BlockSpec((bb, Lp, c), lambda i: (i, 0, 0))
    st_in = pl.BlockSpec((None, bb, SI, SSD_N), lambda i: (l_in, i, 0, 0))
    st_out = pl.BlockSpec((None, bb, SI, SSD_N), lambda i: (l_out, i, 0, 0))
    in_specs = [pl.BlockSpec(memory_space=pltpu.SMEM), blk3(SI), blk3(bc.shape[2]), st_in]
    args = [dec, xw, bc, st_all]
    aliases = {}
    if out_all is not None:
        in_specs.append(pl.BlockSpec(memory_space=pl.ANY))
        args.append(out_all)
        aliases = {4: 1}
        out_stack = jax.ShapeDtypeStruct(out_all.shape, f32)
    else:
        out_stack = jax.ShapeDtypeStruct((depth,) + st_all.shape[1:], f32)
    return pl.pallas_call(
        functools.partial(_s_ssd_kernel, bb=bb, aliased=out_all is not None),
        grid=(Bn // bb,),
        in_specs=in_specs,
        out_specs=[blk3(SI), st_out],
        out_shape=[jax.ShapeDtypeStruct((Bn, Lp, SI), f32), out_stack],
        input_output_aliases=aliases,
        compiler_params=pltpu.CompilerParams(dimension_semantics=("arbitrary",), vmem_limit_bytes=VMEM_LIMIT),
        name="s_ssd",
    )(*args)


def _s_mix_c_kernel(x_ref, lru_ref, ypart_ref, yraw_ref, ecs_ref, z_ref, snorm_ref, wout_ref, o_ref):
    W = lru_ref.shape[1]
    SI = ypart_ref.shape[1]
    ys = ypart_ref[...] + yraw_ref[...] * ecs_ref[...]
    ssd_out = _rms(ys * _silu(z_ref[...]), snorm_ref[...]).astype(bf16)
    o_ref[...] = (x_ref[...] + _dot(lru_ref[...].astype(bf16), wout_ref[0:W, :])
                  + _dot(ssd_out, wout_ref[W:W + SI, :]))


def _s_mix_c(x, lru, ypart, yraw, ecs, z, w, l):
    acts = [x, lru, ypart, yraw, ecs, z]
    return pl.pallas_call(
        _s_mix_c_kernel,
        grid=(1,),
        in_specs=[_full_spec(a) for a in acts] + [_layer_spec(w["snorm"], l), _layer_spec(w["wout"], l)],
        out_specs=_full_spec(x),
        out_shape=jax.ShapeDtypeStruct(x.shape, f32),
        compiler_params=pltpu.CompilerParams(dimension_semantics=("arbitrary",), vmem_limit_bytes=VMEM_LIMIT),
        name="s_mix_c",
    )(*acts, w["snorm"], w["wout"])


def _s_ffn_kernel(x_ref, g_ref, wug_ref, wuv_ref, cwg_ref, cwv_ref, cbg_ref, cbv_ref, wdn_ref, gfin_ref,
                  stg_ref, stv_ref, o_ref, nstg_ref, nstv_ref, hn_sc, acc_sc, *, L, Bn, final):
    c = pl.program_id(0)

    @pl.when(c == 0)
    def _():
        hn_sc[...] = _rms(x_ref[...], g_ref[...]).astype(bf16)
        acc_sc[...] = jnp.zeros(acc_sc.shape, f32)

    hn = hn_sc[...]
    yg, nstg = _tm_conv(stg_ref[...], _dot(hn, wug_ref[...]), cwg_ref[...], cbg_ref[...], Bn, L)
    yv, nstv = _tm_conv(stv_ref[...], _dot(hn, wuv_ref[...]), cwv_ref[...], cbv_ref[...], Bn, L)
    nstg_ref[...] = nstg
    nstv_ref[...] = nstv
    acc_sc[...] += _dot((jax.nn.gelu(yg) * yv).astype(bf16), wdn_ref[...])

    @pl.when(c == pl.num_programs(0) - 1)
    def _():
        out = x_ref[...] + acc_sc[...]
        if final:
            out = _rms(out, gfin_ref[...])
        o_ref[...] = out


def _s_ffn(x, w, l, st_fc, gfin, *, L, Bn, FC, final):
    R, D = x.shape
    F = w["wdn"].shape[1]
    NC = F // FC
    lcolg = lambda rows: pl.BlockSpec((None, rows, FC), lambda c: (l, 0, c))
    lcolv = lambda rows: pl.BlockSpec((None, rows, FC), lambda c: (l, 0, c + NC))
    colg = lambda rows: pl.BlockSpec((rows, FC), lambda c: (0, c))
    colv = lambda rows: pl.BlockSpec((rows, FC), lambda c: (0, c + NC))
    return pl.pallas_call(
        functools.partial(_s_ffn_kernel, L=L, Bn=Bn, final=final),
        grid=(NC,),
        in_specs=[_full_spec(x), pl.BlockSpec((None, 1, D), lambda c: (l, 0, 0)),
                  lcolg(D), lcolv(D), lcolg(3), lcolv(3), lcolg(1), lcolv(1),
                  pl.BlockSpec((None, FC, D), lambda c: (l, c, 0)), _full_spec(gfin), colg(2 * Bn), colv(2 * Bn)],
        out_specs=[_full_spec(x), colg(2 * Bn), colg(2 * Bn)],
        out_shape=[jax.ShapeDtypeStruct((R, D), f32), jax.ShapeDtypeStruct((2 * Bn, F), f32),
                   jax.ShapeDtypeStruct((2 * Bn, F), f32)],
        scratch_shapes=[pltpu.VMEM((R, D), bf16), pltpu.VMEM((R, D), f32)],
        compiler_params=pltpu.CompilerParams(dimension_semantics=("arbitrary",), vmem_limit_bytes=VMEM_LIMIT),
        name="s_ffn",
    )(x, w["gffn"], w["wup"], w["wup"], w["cw"], w["cw"], w["cb"], w["cb"], w["wdn"], gfin, st_fc, st_fc)


def _to_tm(a):
    Bn, K, C = a.shape
    return jnp.swapaxes(a, 0, 1).reshape(K * Bn, C)


def _from_tm(a, Bn):
    K = a.shape[0] // Bn
    return jnp.swapaxes(a.reshape(K, Bn, a.shape[1]), 0, 1)


def _short_layer(x, w, l, st, ss_all, ss_l, ss_out, gfin, *, L, Bn, final):
    (lru, z, bc, ypart, ecs, xw, dec, n_lc, n_lh, n_sc) = _s_mix_a(x, w, l, st, L=L, Bn=Bn)
    Lp = -(-L // SUBLANES) * SUBLANES
    bm = lambda a: jnp.pad(_from_tm(a, Bn), ((0, 0), (0, Lp - L), (0, 0)))
    yraw, ss_out = _s_ssd(dec[:, :HEADS].reshape(-1), bm(xw), bm(bc), ss_all, ss_l, ss_out, l, bb=SUBLANES,
                          depth=w["win"].shape[0])
    x = _s_mix_c(x, lru, ypart, _to_tm(yraw[:, :L]), ecs, z, w, l)
    x, nfg, nfv = _s_ffn(x, w, l, st["fc"], gfin, L=L, Bn=Bn, FC=512, final=final)
    new = dict(lc=n_lc, lh=n_lh, sc=n_sc, fc=jnp.concatenate([nfg, nfv], axis=1))
    return x, new, ss_out


def _prep_weights(norm_mix, w_in, lru_conv_w, lru_conv_b, lru_wa, lru_ba, lru_wx, lru_bx, lru_a_param,
                  lru_out_norm, ssd_conv_w, ssd_conv_b, ssd_dt_bias, ssd_a_log, ssd_d, ssd_out_norm, w_out,
                  norm_ffn, ffn_w_up, ffn_conv_w, ffn_conv_b, ffn_w_down):
    depth = w_in.shape[0]
    W = lru_conv_w.shape[2]
    CD = ssd_conv_w.shape[2]
    SI = HEADS * HEAD_P
    o3, o4 = 2 * W + SI, 2 * W + SI + CD
    win = jnp.concatenate([w_in[:, :, o3:o4], w_in[:, :, 0:o3],
                           jnp.pad(w_in[:, :, o4:], ((0, 0), (0, 0), (0, LANES - HEADS)))], axis=2).astype(bf16)
    hd = W // HEADS
    per = 256 // hd
    nblk = HEADS // per

    def blockdiag(wh):
        wh = wh.reshape(depth, nblk, per, hd, hd)
        eye = jnp.eye(per, dtype=wh.dtype)
        return (eye[None, None, :, None, :, None] * wh[:, :, :, :, None, :]).reshape(depth, nblk, per * hd, per * hd)

    wg = jnp.concatenate([blockdiag(lru_wa), blockdiag(lru_wx)], axis=3).astype(bf16)
    row = lambda v: v.reshape(depth, 1, -1).astype(f32)
    pad_h = lambda v: jnp.pad(v.astype(f32), ((0, 0), (0, LANES - HEADS))).reshape(depth, 1, LANES)
    return dict(
        gmix=row(norm_mix), win=win, lcw=lru_conv_w, lcb=row(lru_conv_b), wg=wg,
        bg=jnp.stack([lru_ba, lru_bx], axis=1), ap=row(lru_a_param), lnorm=row(lru_out_norm),
        scw=ssd_conv_w, scb=row(ssd_conv_b), dtb=pad_h(ssd_dt_bias), alog=pad_h(ssd_a_log),
        dvec=row(jnp.repeat(ssd_d, HEAD_P, axis=1)), snorm=row(ssd_out_norm), wout=w_out.astype(bf16),
        gffn=row(norm_ffn), wup=ffn_w_up.astype(bf16), cw=ffn_conv_w, cb=row(ffn_conv_b),
        wdn=ffn_w_down.astype(bf16))


def _tile(S, cap):
    T = min(S, cap)
    while S % T:
        T -= CHUNK
    return T


def kernel(x_prompt, x_sample, state_lru_conv, state_lru_h, state_ssd_conv, state_ssd, state_ffn_conv, meta_tokens, norm_mix, w_in, lru_conv_w, lru_conv_b, lru_wa, lru_ba, lru_wx, lru_bx, lru_a_param, lru_out_norm, ssd_conv_w, ssd_conv_b, ssd_dt_bias, ssd_a_log, ssd_d, ssd_out_norm, w_out, norm_ffn, ffn_w_up, ffn_conv_w, ffn_conv_b, ffn_w_down, norm_final):
    B, S, D = x_prompt.shape
    Bs, Ls, _ = x_sample.shape
    depth = w_in.shape[0]
    n_meta = meta_tokens.shape[0]
    W = lru_conv_w.shape[2]
    CD = ssd_conv_w.shape[2]
    SI = HEADS * HEAD_P
    F2 = ffn_w_up.shape[2]
    w = _prep_weights(norm_mix, w_in, lru_conv_w, lru_conv_b, lru_wa, lru_ba, lru_wx, lru_bx, lru_a_param,
                      lru_out_norm, ssd_conv_w, ssd_conv_b, ssd_dt_bias, ssd_a_log, ssd_d, ssd_out_norm, w_out,
                      norm_ffn, ffn_w_up, ffn_conv_w, ffn_conv_b, ffn_w_down)
    gfin = norm_final.reshape(1, D).astype(f32)

    xm = jnp.broadcast_to(meta_tokens.astype(f32)[:, None, :], (n_meta, B, D)).reshape(n_meta * B, D)
    zero = dict(lc=jnp.zeros((3 * B, W), f32), lh=jnp.zeros((B, W), f32), sc=jnp.zeros((3 * B, CD), f32),
                fc=jnp.zeros((2 * B, F2), f32))
    zero_ss = jnp.zeros((1, B, SI, SSD_N), f32)
    meta_st, meta_ss = [], None
    for l in range(depth):
        xm, new, meta_ss = _short_layer(xm, w, l, zero, zero_ss, 0, meta_ss, gfin, L=n_meta, Bn=B, final=False)
        meta_st.append(new)

    Tm = _tile(S, 256)
    Tf = _tile(S, 512)
    xp = x_prompt
    p_lc, p_lh, p_sc, p_ss, p_fc = [], [], [], [], []
    for l in range(depth):
        m = meta_st[l]
        st = dict(lc=_from_tm(m["lc"], B), lh=m["lh"].reshape(B, 1, W), sc=_from_tm(m["sc"], B), ss=meta_ss[l])
        xp, lc, lh, sc, ss = _p_mixer(xp, w, l, st, T=Tm, NB=2)
        xp, fc = _p_ffn(xp, w, l, _from_tm(m["fc"], B), gfin, T=Tf, FC=512, final=(l == depth - 1))
        p_lc.append(lc); p_lh.append(lh.reshape(B, W)); p_sc.append(sc)
        p_ss.append(ss.reshape(B, HEADS, HEAD_P, SSD_N)); p_fc.append(fc)

    xs = _to_tm(x_sample)
    ss_in = state_ssd.reshape(depth, Bs, SI, SSD_N)
    s_ss = None
    s_lc, s_lh, s_sc, s_fc = [], [], [], []
    for l in range(depth):
        st = dict(lc=_to_tm(state_lru_conv[l]), lh=state_lru_h[l], sc=_to_tm(state_ssd_conv[l]),
                  fc=_to_tm(state_ffn_conv[l]))
        xs, new, s_ss = _short_layer(xs, w, l, st, ss_in, l, s_ss, gfin, L=Ls, Bn=Bs, final=(l == depth - 1))
        s_lc.append(_from_tm(new["lc"], Bs)); s_lh.append(new["lh"]); s_sc.append(_from_tm(new["sc"], Bs))
        s_fc.append(_from_tm(new["fc"], Bs))

    st = jnp.stack
    return (xp, _from_tm(xs, Bs), st(p_lc), st(p_lh), st(p_sc), st(p_ss), st(p_fc),
            st(s_lc), st(s_lh), st(s_sc), s_ss.reshape(depth, Bs, HEADS, HEAD_P, SSD_N), st(s_fc))
```

```python
import functools

import jax
import jax.numpy as jnp
from jax import lax
from jax.experimental import pallas as pl
from jax.experimental.pallas import tpu as pltpu

f32 = jnp.float32
bf16 = jnp.bfloat16

EPS = 1e-6
LRU_C = 8.0
HEADS = 8
HEAD_P = 64
SSD_N = 128
CHUNK = 128
LANES = 128
SUBLANES = 8
VMEM_LIMIT = 56 * 1024 * 1024
HI = lax.Precision.HIGHEST


def _rms(x, g):
    return x * lax.rsqrt(jnp.mean(x * x, axis=-1, keepdims=True) + EPS) * g


def _silu(x):
    return x * jax.nn.sigmoid(x)


def _softplus(x):
    return jnp.maximum(x, 0.0) + jnp.log1p(jnp.exp(-jnp.abs(x)))


def _dot(a, b):
    return jnp.dot(a, b, preferred_element_type=f32)


def _lru_coeffs(xc, pre_r, pre_i, a_param):
    r = jax.nn.sigmoid(pre_r)
    i = jax.nn.sigmoid(pre_i)
    log_a = -LRU_C * r * _softplus(-a_param)
    a = jnp.exp(log_a)
    th = jnp.tanh(log_a)
    mult = jnp.sqrt(-2.0 * th / (1.0 - th))
    return a, mult * (i * xc)


def _gates(xc, wg_ref, bg_ref):
    half = wg_ref.shape[1]
    pr, pi = [], []
    for j in range(wg_ref.shape[0]):
        ri = _dot(xc[:, j * half:(j + 1) * half].astype(bf16), wg_ref[j])
        pr.append(ri[:, :half])
        pi.append(ri[:, half:])
    bg = bg_ref[...]
    return jnp.concatenate(pr, axis=1) + bg[0:1], jnp.concatenate(pi, axis=1) + bg[1:2]


def _head_expand():
    r = lax.broadcasted_iota(jnp.int32, (LANES, HEADS * HEAD_P), 0)
    c = lax.broadcasted_iota(jnp.int32, (LANES, HEADS * HEAD_P), 1)
    return (r == lax.shift_right_logical(c, 6)).astype(f32)


def _layer_spec(a, l):
    nd = a.ndim - 1
    return pl.BlockSpec((None,) + a.shape[1:], lambda *_: (l,) + (0,) * nd, pipeline_mode=pl.Buffered(1))


def _full_spec(a):
    nd = a.ndim
    return pl.BlockSpec(a.shape, lambda *_: (0,) * nd)


MIX_KEYS = ("gmix", "win", "lcw", "lcb", "wg", "bg", "ap", "lnorm", "scw", "scb", "dtb", "alog", "dvec")


def _scan_rows(a, b, h0):
    T, C = a.shape
    G = T // SUBLANES
    a3 = a.reshape(G, SUBLANES, C)
    b3 = b.reshape(G, SUBLANES, C)
    sub = lax.broadcasted_iota(jnp.int32, a3.shape, 1)
    for d in (1, 2, 4):
        ar = pltpu.roll(a3, d, axis=1)
        br = pltpu.roll(b3, d, axis=1)
        m = sub >= d
        b3 = jnp.where(m, a3 * br + b3, b3)
        a3 = jnp.where(m, a3 * ar, a3)
    hs = []
    hp = h0
    for gi in range(G):
        hb = a3[gi] * hp + b3[gi]
        hs.append(hb)
        hp = hb[SUBLANES - 1:SUBLANES, :]
    return jnp.concatenate(hs, axis=0), hp


def _p_mixer_kernel(x_ref, gmix_ref, win_ref, lcw_ref, lcb_ref, wg_ref, bg_ref, ap_ref, lnorm_ref,
                    scw_ref, scb_ref, dtb_ref, alog_ref, dvec_ref, snorm_ref, wout_ref,
                    st_lc_ref, st_lh_ref, st_sc_ref, st_ss_ref,
                    o_ref, n_lc_ref, n_lh_ref, n_sc_ref, n_ss_ref,
                    ext_l, ext_s, hcar, s_pair, ys_sc, xb_sc, dt_sc, gz_sc, mix_sc, hn_sc, *, T, NB):
    t = pl.program_id(1)
    nt = pl.num_programs(1)
    W = ext_l.shape[2]
    CD = ext_s.shape[2]
    SI = HEADS * HEAD_P
    NP = HEADS // 2

    @pl.when(t == 0)
    def _():
        for s in range(NB):
            ext_l[s, 5:8, :] = st_lc_ref[s]
            ext_s[s, 5:8, :] = st_sc_ref[s]
            hcar[s, 0:1, :] = st_lh_ref[s]
            for j in range(NP):
                s_pair[s, j] = st_ss_ref[s, j * LANES:(j + 1) * LANES, :].T

    a_row = -jnp.exp(alog_ref[...])
    row = lax.broadcasted_iota(jnp.int32, (CHUNK, CHUNK), 0)
    col = lax.broadcasted_iota(jnp.int32, (CHUNK, CHUNK), 1)
    causal = col <= row
    ltri = causal.astype(f32)
    lane_l = col < HEAD_P
    eexp = _head_expand()
    dvec = dvec_ref[...]
    lcw = lcw_ref[...]
    scw = scw_ref[...]

    def stage_a(s):
        def norm():
            hn_sc[s] = _rms(x_ref[s], gmix_ref[...]).astype(bf16)

        def proj(dst, d0, c0, n):
            def step():
                dst[s, d0[0]:d0[0] + T, d0[1]:d0[1] + n] = _dot(hn_sc[s], win_ref[:, c0:c0 + n])
            return step

        def dt():
            dt_sc[s] = _softplus(_dot(hn_sc[s], win_ref[:, CD + 3 * W:CD + 3 * W + LANES]) + dtb_ref[...])

        steps = [norm, proj(ext_l, (8, 0), CD, W)]
        steps += [proj(ext_s, (8, c), c, W) for c in range(0, CD, W)]
        steps += [proj(gz_sc, (0, 0), CD + W, W), proj(gz_sc, (0, W), CD + 2 * W, W), dt]
        return steps

    def stage_b(s):
        xc = lcb_ref[...] + lcw[3:4] * ext_l[s, 8:T + 8, :]
        for k in range(3):
            xc = xc + lcw[k:k + 1] * ext_l[s, 5 + k:5 + k + T, :]
        ext_l[s, 5:8, :] = ext_l[s, T + 5:T + 8, :]
        pre_r, pre_i = _gates(xc, wg_ref, bg_ref)
        a, bt = _lru_coeffs(xc, pre_r, pre_i, ap_ref[...])
        yield
        hl, hlast = _scan_rows(a, bt, hcar[s, 0:1, :])
        hcar[s, 0:1, :] = hlast
        yield
        mix_sc[s, :, 0:W] = _rms(hl * jax.nn.gelu(gz_sc[s, :, 0:W]), lnorm_ref[...]).astype(bf16)
        yield

        xb = scb_ref[...] + scw[3:4] * ext_s[s, 8:T + 8, :]
        for k in range(3):
            xb = xb + scw[k:k + 1] * ext_s[s, 5 + k:5 + k + T, :]
        ext_s[s, 5:8, :] = ext_s[s, T + 5:T + 8, :]
        xb_sc[s] = _silu(xb)
        yield

        for q in range(T // CHUNK):
            r0 = q * CHUNK
            xs_c = xb_sc[s, r0:r0 + CHUNK, 0:SI]
            dt_c = dt_sc[s, r0:r0 + CHUNK, :]
            cs_col = jnp.dot(ltri, dt_c * a_row, precision=HI, preferred_element_type=f32)
            cs_row = cs_col.T[0:HEADS]
            dt_row = dt_c.T[0:HEADS]
            ecol = jnp.exp(cs_col)
            w8 = jnp.exp(cs_row[:, CHUNK - 1:CHUNK] - cs_row) * dt_row
            dec_x = jnp.exp(jnp.dot(cs_col[CHUNK - SUBLANES:CHUNK], eexp, precision=HI,
                                    preferred_element_type=f32)[SUBLANES - 1:SUBLANES])
            for g in range(2):
                bm_t = xb_sc[s, r0:r0 + CHUNK, SI + g * SSD_N:SI + (g + 1) * SSD_N].T
                cm = xb_sc[s, r0:r0 + CHUNK, SI + 2 * SSD_N + g * SSD_N:SI + 2 * SSD_N + (g + 1) * SSD_N]
                scores = _dot(cm.astype(bf16), bm_t.astype(bf16))
                for jj in range(2):
                    j = 2 * g + jj
                    lhs, lhs2 = [], []
                    for h in (2 * j, 2 * j + 1):
                        lmat = jnp.where(causal, jnp.exp(cs_col[:, h:h + 1] - cs_row[h:h + 1]), 0.0)
                        lhs.append(scores * lmat * dt_row[h:h + 1])
                        lhs.append(cm * ecol[:, h:h + 1])
                        lhs2.append(bm_t * w8[h:h + 1])
                    xp = xs_c[:, j * LANES:(j + 1) * LANES]
                    sp = s_pair[s, j]
                    xl = jnp.where(lane_l, xp, 0.0)
                    xr = jnp.where(lane_l, 0.0, xp)
                    rhs = jnp.concatenate([xl, jnp.where(lane_l, sp, 0.0), xr, jnp.where(lane_l, 0.0, sp)], axis=0)
                    y = _dot(jnp.concatenate(lhs, axis=1).astype(bf16), rhs.astype(bf16))
                    upd = _dot(jnp.concatenate(lhs2, axis=1).astype(bf16),
                               jnp.concatenate([xl, xr], axis=0).astype(bf16))
                    s_pair[s, j] = dec_x[:, j * LANES:(j + 1) * LANES] * sp + upd
                    ys_sc[s, r0:r0 + CHUNK, j * LANES:(j + 1) * LANES] = y + xp * dvec[:, j * LANES:(j + 1) * LANES]
                yield

        mix_sc[s, :, W:W + SI] = _rms(ys_sc[s] * _silu(gz_sc[s, :, W:2 * W]), snorm_ref[...]).astype(bf16)

    def stage_c(s):
        o_ref[s] = x_ref[s] + _dot(mix_sc[s], wout_ref[...])

    for step in stage_a(0):
        step()
    for s in range(NB):
        pending = stage_a(s + 1) if s + 1 < NB else []
        for _ in stage_b(s):
            if pending:
                pending.pop(0)()
        for step in pending:
            step()
        stage_c(s)

    @pl.when(t == nt - 1)
    def _():
        for s in range(NB):
            n_lc_ref[s] = ext_l[s, 5:8, :]
            n_sc_ref[s] = ext_s[s, 5:8, :]
            n_lh_ref[s] = hcar[s, 0:1, :]
            for j in range(NP):
                n_ss_ref[s, j * LANES:(j + 1) * LANES, :] = s_pair[s, j].T


def _p_mixer(x, w, l, st, *, T, NB):
    B, S, D = x.shape
    W = w["lcw"].shape[-1]
    CD = w["scw"].shape[-1]
    SI = HEADS * HEAD_P
    consts = [w[k] for k in MIX_KEYS] + [w["snorm"], w["wout"]]
    st_specs = [pl.BlockSpec((NB, 3, W), lambda b, t: (b, 0, 0)),
                pl.BlockSpec((NB, 1, W), lambda b, t: (b, 0, 0)),
                pl.BlockSpec((NB, 3, CD), lambda b, t: (b, 0, 0)),
                pl.BlockSpec((NB, SI, SSD_N), lambda b, t: (b, 0, 0))]
    x_spec = pl.BlockSpec((NB, T, D), lambda b, t: (b, t, 0))
    return pl.pallas_call(
        functools.partial(_p_mixer_kernel, T=T, NB=NB),
        grid=(B // NB, S // T),
        in_specs=[x_spec] + [_layer_spec(c, l) for c in consts] + st_specs,
        out_specs=[x_spec] + st_specs,
        out_shape=[jax.ShapeDtypeStruct((B, S, D), f32),
                   jax.ShapeDtypeStruct((B, 3, W), f32), jax.ShapeDtypeStruct((B, 1, W), f32),
                   jax.ShapeDtypeStruct((B, 3, CD), f32), jax.ShapeDtypeStruct((B, SI, SSD_N), f32)],
        scratch_shapes=[pltpu.VMEM((NB, T + 8, W), f32), pltpu.VMEM((NB, T + 8, CD), f32),
                        pltpu.VMEM((NB, SUBLANES, W), f32), pltpu.VMEM((NB, HEADS // 2, SSD_N, LANES), f32),
                        pltpu.VMEM((NB, T, SI), f32), pltpu.VMEM((NB, T, CD), f32), pltpu.VMEM((NB, T, LANES), f32),
                        pltpu.VMEM((NB, T, 2 * W), f32), pltpu.VMEM((NB, T, W + SI), bf16),
                        pltpu.VMEM((NB, T, D), bf16)],
        compiler_params=pltpu.CompilerParams(dimension_semantics=("arbitrary", "arbitrary"),
                                             vmem_limit_bytes=VMEM_LIMIT),
        name="p_mixer",
    )(x, *consts, st["lc"], st["lh"], st["sc"], st["ss"])


def _p_ffn_kernel(x_ref, g_ref, wup_ref, cw_ref, cb_ref, wdn_ref, gfin_ref, st_ref,
                  o_ref, nst_ref, carry, *, TT, FC, final):
    i = pl.program_id(0)
    B, _, D = x_ref.shape
    R = TT * B
    F = wdn_ref.shape[0]

    @pl.when(i == 0)
    def _():
        carry[...] = st_ref[...]

    x = jnp.swapaxes(x_ref[...], 0, 1).reshape(R, D)
    hn = _rms(x, g_ref[...]).astype(bf16)
    acc = jnp.zeros(x.shape, f32)
    for c in range(F // FC):
        ys = []
        for half in range(2):
            c0 = half * F + c * FC
            u = _dot(hn, wup_ref[:, c0:c0 + FC])
            ext = jnp.concatenate([carry[:, c0:c0 + FC], u], axis=0)
            cw = cw_ref[:, c0:c0 + FC]
            ys.append(cb_ref[:, c0:c0 + FC] + cw[0:1] * ext[0:R] + cw[1:2] * ext[B:B + R] + cw[2:3] * u)
            carry[:, c0:c0 + FC] = ext[R:R + 2 * B]
        act = (jax.nn.gelu(ys[0]) * ys[1]).astype(bf16)
        acc = acc + _dot(act, wdn_ref[c * FC:(c + 1) * FC, :])
    out = x + acc
    if final:
        out = _rms(out, gfin_ref[...])
    o_ref[...] = jnp.swapaxes(out.reshape(TT, B, D), 0, 1)

    @pl.when(i == pl.num_programs(0) - 1)
    def _():
        nst_ref[...] = carry[...]


def _p_ffn(x, w, l, st_fc, gfin, *, TT, FC, final):
    B, S, D = x.shape
    F2 = w["wup"].shape[-1]
    assert B % SUBLANES == 0, "time-major row tiles need whole sublane groups per time step"
    consts = [w["gffn"], w["wup"], w["cw"], w["cb"], w["wdn"]]
    x_spec = pl.BlockSpec((B, TT, D), lambda i: (0, i, 0))
    return pl.pallas_call(
        functools.partial(_p_ffn_kernel, TT=TT, FC=FC, final=final),
        grid=(S // TT,),
        in_specs=[x_spec] + [_layer_spec(c, l) for c in consts] + [_full_spec(gfin), _full_spec(st_fc)],
        out_specs=[x_spec, _full_spec(st_fc)],
        out_shape=[jax.ShapeDtypeStruct((B, S, D), f32), jax.ShapeDtypeStruct((2 * B, F2), f32)],
        scratch_shapes=[pltpu.VMEM((2 * B, F2), f32)],
        compiler_params=pltpu.CompilerParams(dimension_semantics=("arbitrary",), vmem_limit_bytes=VMEM_LIMIT),
        name="p_ffn",
    )(x, *consts, gfin, st_fc)


def _tm_conv(state, u, w, b, n_rows, L):
    K = w.shape[0]
    ext = jnp.concatenate([state, u], axis=0)
    R = L * n_rows
    y = b
    for k in range(K):
        y = y + w[k:k + 1] * ext[k * n_rows:k * n_rows + R]
    return y, ext[R:R + (K - 1) * n_rows]


def _s_mix_a_kernel(x_ref, gmix_ref, win_ref, lcw_ref, lcb_ref, wg_ref, bg_ref, ap_ref, lnorm_ref,
                    scw_ref, scb_ref, dtb_ref, alog_ref, dvec_ref,
                    st_lc_ref, st_lh_ref, st_sc_ref,
                    lru_ref, z_ref, bc_ref, ypart_ref, ecs_ref, xw_ref, dec_ref,
                    n_lc_ref, n_lh_ref, n_sc_ref, *, L, Bn):
    W = lcw_ref.shape[1]
    CD = scw_ref.shape[1]
    SI = HEADS * HEAD_P
    hn = _rms(x_ref[...], gmix_ref[...]).astype(bf16)

    xc, n_lc = _tm_conv(st_lc_ref[...], _dot(hn, win_ref[:, CD:CD + W]), lcw_ref[...], lcb_ref[...], Bn, L)
    n_lc_ref[...] = n_lc
    pre_r, pre_i = _gates(xc, wg_ref, bg_ref)
    a, bt = _lru_coeffs(xc, pre_r, pre_i, ap_ref[...])
    hp = st_lh_ref[...]
    hs = []
    for t in range(L):
        hp = a[t * Bn:(t + 1) * Bn] * hp + bt[t * Bn:(t + 1) * Bn]
        hs.append(hp)
    n_lh_ref[...] = hp
    gate = _dot(hn, win_ref[:, CD + W:CD + 2 * W])
    lru_ref[...] = _rms(jnp.concatenate(hs, axis=0) * jax.nn.gelu(gate), lnorm_ref[...])
    z_ref[...] = _dot(hn, win_ref[:, CD + 2 * W:CD + 3 * W])

    xb, n_sc = _tm_conv(st_sc_ref[...], _dot(hn, win_ref[:, 0:CD]), scw_ref[...], scb_ref[...], Bn, L)
    n_sc_ref[...] = n_sc
    xb = _silu(xb)
    xs = xb[:, 0:SI]
    bc_ref[...] = xb[:, SI:CD]
    dt = _softplus(_dot(hn, win_ref[:, CD + 3 * W:CD + 3 * W + LANES]) + dtb_ref[...])
    dA = dt * (-jnp.exp(alog_ref[...]))
    css = []
    cs = jnp.zeros((Bn, LANES), f32)
    for t in range(L):
        cs = cs + dA[t * Bn:(t + 1) * Bn]
        css.append(cs)
    dec_ref[...] = jnp.exp(cs)
    eexp = _head_expand()
    csx = jnp.dot(jnp.concatenate(css, axis=0), eexp, precision=HI, preferred_element_type=f32)
    dtx = jnp.dot(dt, eexp, precision=HI, preferred_element_type=f32)
    ecs_ref[...] = jnp.exp(csx)
    cs_end = csx[(L - 1) * Bn:L * Bn]
    dvec = dvec_ref[...]
    half = SI // 2
    for t in range(L):
        rt = slice(t * Bn, (t + 1) * Bn)
        xw_ref[rt, :] = xs[rt] * jnp.exp(cs_end - csx[rt]) * dtx[rt]
        acc = dvec * xs[rt]
        cm = xb[rt, SI + 2 * SSD_N:CD]
        for s in range(t + 1):
            rs = slice(s * Bn, (s + 1) * Bn)
            prod = cm * xb[rs, SI:SI + 2 * SSD_N]
            gx = jnp.concatenate(
                [jnp.broadcast_to(jnp.sum(prod[:, 0:SSD_N], axis=-1, keepdims=True), (Bn, half)),
                 jnp.broadcast_to(jnp.sum(prod[:, SSD_N:2 * SSD_N], axis=-1, keepdims=True), (Bn, half))], axis=1)
            acc = acc + gx * jnp.exp(csx[rt] - csx[rs]) * dtx[rs] * xs[rs]
        ypart_ref[rt, :] = acc


def _s_mix_a(x, w, l, st, *, L, Bn):
    R, D = x.shape
    W = w["lcw"].shape[-1]
    CD = w["scw"].shape[-1]
    SI = HEADS * HEAD_P
    sds = jax.ShapeDtypeStruct
    consts = [w[k] for k in MIX_KEYS]
    acts = [st["lc"], st["lh"], st["sc"]]
    out_shape = [sds((R, W), f32), sds((R, SI), f32), sds((R, CD - SI), f32), sds((R, SI), f32),
                 sds((R, SI), f32), sds((R, SI), f32), sds((Bn, LANES), f32),
                 sds((3 * Bn, W), f32), sds((Bn, W), f32), sds((3 * Bn, CD), f32)]
    return pl.pallas_call(
        functools.partial(_s_mix_a_kernel, L=L, Bn=Bn),
        grid=(1,),
        in_specs=[_full_spec(x)] + [_layer_spec(c, l) for c in consts] + [_full_spec(a) for a in acts],
        out_specs=[_full_spec(o) for o in out_shape],
        out_shape=out_shape,
        compiler_params=pltpu.CompilerParams(dimension_semantics=("arbitrary",), vmem_limit_bytes=VMEM_LIMIT),
        name="s_mix_a",
    )(x, *consts, *acts)


def _s_ssd_kernel(*refs, bb, aliased):
    if aliased:
        dec_ref, xw_ref, bc_ref, st_ref, _, yraw_ref, nst_ref = refs
    else:
        dec_ref, xw_ref, bc_ref, st_ref, yraw_ref, nst_ref = refs
    i = pl.program_id(0)
    GW = (HEADS // 2) * HEAD_P
    for j in range(bb):
        xw = xw_ref[j].astype(bf16)
        bc = bc_ref[j].astype(bf16)
        ys = []
        for g in range(2):
            sg = st_ref[j, g * GW:(g + 1) * GW, :]
            upd = lax.dot_general(xw[:, g * GW:(g + 1) * GW], bc[:, g * SSD_N:(g + 1) * SSD_N],
                                  (((0,), (0,)), ((), ())), preferred_element_type=f32)
            cg = bc[:, 2 * SSD_N + g * SSD_N:2 * SSD_N + (g + 1) * SSD_N]
            ys.append(lax.dot_general(cg, sg.astype(bf16), (((1,), (1,)), ((), ())),
                                      preferred_element_type=f32))
            for hh in range(HEADS // 2):
                h = g * (HEADS // 2) + hh
                r0 = g * GW + hh * HEAD_P
                nst_ref[j, r0:r0 + HEAD_P, :] = (dec_ref[(i * bb + j) * HEADS + h] * sg[hh * HEAD_P:(hh + 1) * HEAD_P, :]
                                                 + upd[hh * HEAD_P:(hh + 1) * HEAD_P, :])
        yraw_ref[j] = jnp.concatenate(ys, axis=1)


def _s_ssd(dec, xw, bc, st_all, l_in, out_all, l_out, *, bb, depth):
    Bn, Lp, SI = xw.shape
    blk3 = lambda c: pl.BlockSpec((bb, Lp, c), lambda i: (i, 0, 0))
    st_in = pl.BlockSpec((None, bb, SI, SSD_N), lambda i: (l_in, i, 0, 0))
    st_out = pl.BlockSpec((None, bb, SI, SSD_N), lambda i: (l_out, i, 0, 0))
    in_specs = [pl.BlockSpec(memory_space=pltpu.SMEM), blk3(SI), blk3(bc.shape[2]), st_in]
    args = [dec, xw, bc, st_all]
    aliases = {}
    if out_all is not None:
        in_specs.append(pl.BlockSpec(memory_space=pl.ANY))
        args.append(out_all)
        aliases = {4: 1}
        out_stack = jax.ShapeDtypeStruct(out_all.shape, f32)
    else:
        out_stack = jax.ShapeDtypeStruct((depth,) + st_all.shape[1:], f32)
    return pl.pallas_call(
        functools.partial(_s_ssd_kernel, bb=bb, aliased=out_all is not None),
        grid=(Bn // bb,),
        in_specs=in_specs,
        out_specs=[blk3(SI), st_out],
        out_shape=[jax.ShapeDtypeStruct((Bn, Lp, SI), f32), out_stack],
        input_output_aliases=aliases,
        compiler_params=pltpu.CompilerParams(dimension_semantics=("arbitrary",), vmem_limit_bytes=VMEM_LIMIT),
        name="s_ssd",
    )(*args)


def _s_mix_c_kernel(x_ref, lru_ref, ypart_ref, yraw_ref, ecs_ref, z_ref, snorm_ref, wout_ref, o_ref):
    W = lru_ref.shape[1]
    SI = ypart_ref.shape[1]
    ys = ypart_ref[...] + yraw_ref[...] * ecs_ref[...]
    ssd_out = _rms(ys * _silu(z_ref[...]), snorm_ref[...]).astype(bf16)
    o_ref[...] = (x_ref[...] + _dot(lru_ref[...].astype(bf16), wout_ref[0:W, :])
                  + _dot(ssd_out, wout_ref[W:W + SI, :]))


def _s_mix_c(x, lru, ypart, yraw, ecs, z, w, l):
    acts = [x, lru, ypart, yraw, ecs, z]
    return pl.pallas_call(
        _s_mix_c_kernel,
        grid=(1,),
        in_specs=[_full_spec(a) for a in acts] + [_layer_spec(w["snorm"], l), _layer_spec(w["wout"], l)],
        out_specs=_full_spec(x),
        out_shape=jax.ShapeDtypeStruct(x.shape, f32),
        compiler_params=pltpu.CompilerParams(dimension_semantics=("arbitrary",), vmem_limit_bytes=VMEM_LIMIT),
        name="s_mix_c",
    )(*acts, w["snorm"], w["wout"])


def _s_ffn_kernel(x_ref, g_ref, wug_ref, wuv_ref, cwg_ref, cwv_ref, cbg_ref, cbv_ref, wdn_ref, gfin_ref,
                  stg_ref, stv_ref, o_ref, nstg_ref, nstv_ref, hn_sc, acc_sc, *, L, Bn, final):
    c = pl.program_id(0)

    @pl.when(c == 0)
    def _():
        hn_sc[...] = _rms(x_ref[...], g_ref[...]).astype(bf16)
        acc_sc[...] = jnp.zeros(acc_sc.shape, f32)

    hn = hn_sc[...]
    yg, nstg = _tm_conv(stg_ref[...], _dot(hn, wug_ref[...]), cwg_ref[...], cbg_ref[...], Bn, L)
    yv, nstv = _tm_conv(stv_ref[...], _dot(hn, wuv_ref[...]), cwv_ref[...], cbv_ref[...], Bn, L)
    nstg_ref[...] = nstg
    nstv_ref[...] = nstv
    acc_sc[...] += _dot((jax.nn.gelu(yg) * yv).astype(bf16), wdn_ref[...])

    @pl.when(c == pl.num_programs(0) - 1)
    def _():
        out = x_ref[...] + acc_sc[...]
        if final:
            out = _rms(out, gfin_ref[...])
        o_ref[...] = out


def _s_ffn(x, w, l, st_fc, gfin, *, L, Bn, FC, final):
    R, D = x.shape
    F = w["wdn"].shape[1]
    NC = F // FC
    lcolg = lambda rows: pl.BlockSpec((None, rows, FC), lambda c: (l, 0, c))
    lcolv = lambda rows: pl.BlockSpec((None, rows, FC), lambda c: (l, 0, c + NC))
    colg = lambda rows: pl.BlockSpec((rows, FC), lambda c: (0, c))
    colv = lambda rows: pl.BlockSpec((rows, FC), lambda c: (0, c + NC))
    return pl.pallas_call(
        functools.partial(_s_ffn_kernel, L=L, Bn=Bn, final=final),
        grid=(NC,),
        in_specs=[_full_spec(x), pl.BlockSpec((None, 1, D), lambda c: (l, 0, 0)),
                  lcolg(D), lcolv(D), lcolg(3), lcolv(3), lcolg(1), lcolv(1),
                  pl.BlockSpec((None, FC, D), lambda c: (l, c, 0)), _full_spec(gfin), colg(2 * Bn), colv(2 * Bn)],
        out_specs=[_full_spec(x), colg(2 * Bn), colg(2 * Bn)],
        out_shape=[jax.ShapeDtypeStruct((R, D), f32), jax.ShapeDtypeStruct((2 * Bn, F), f32),
                   jax.ShapeDtypeStruct((2 * Bn, F), f32)],
        scratch_shapes=[pltpu.VMEM((R, D), bf16), pltpu.VMEM((R, D), f32)],
        compiler_params=pltpu.CompilerParams(dimension_semantics=("arbitrary",), vmem_limit_bytes=VMEM_LIMIT),
        name="s_ffn",
    )(x, w["gffn"], w["wup"], w["wup"], w["cw"], w["cw"], w["cb"], w["cb"], w["wdn"], gfin, st_fc, st_fc)


def _to_tm(a):
    Bn, K, C = a.shape
    return jnp.swapaxes(a, 0, 1).reshape(K * Bn, C)


def _from_tm(a, Bn):
    K = a.shape[0] // Bn
    return jnp.swapaxes(a.reshape(K, Bn, a.shape[1]), 0, 1)


def _short_layer(x, w, l, st, ss_all, ss_l, ss_out, gfin, *, L, Bn, final):
    (lru, z, bc, ypart, ecs, xw, dec, n_lc, n_lh, n_sc) = _s_mix_a(x, w, l, st, L=L, Bn=Bn)
    Lp = -(-L // SUBLANES) * SUBLANES
    bm = lambda a: jnp.pad(_from_tm(a, Bn), ((0, 0), (0, Lp - L), (0, 0)))
    yraw, ss_out = _s_ssd(dec[:, :HEADS].reshape(-1), bm(xw), bm(bc), ss_all, ss_l, ss_out, l, bb=SUBLANES,
                          depth=w["win"].shape[0])
    x = _s_mix_c(x, lru, ypart, _to_tm(yraw[:, :L]), ecs, z, w, l)
    x, nfg, nfv = _s_ffn(x, w, l, st["fc"], gfin, L=L, Bn=Bn, FC=512, final=final)
    new = dict(lc=n_lc, lh=n_lh, sc=n_sc, fc=jnp.concatenate([nfg, nfv], axis=1))
    return x, new, ss_out


def _prep_weights(norm_mix, w_in, lru_conv_w, lru_conv_b, lru_wa, lru_ba, lru_wx, lru_bx, lru_a_param,
                  lru_out_norm, ssd_conv_w, ssd_conv_b, ssd_dt_bias, ssd_a_log, ssd_d, ssd_out_norm, w_out,
                  norm_ffn, ffn_w_up, ffn_conv_w, ffn_conv_b, ffn_w_down):
    depth = w_in.shape[0]
    W = lru_conv_w.shape[2]
    CD = ssd_conv_w.shape[2]
    SI = HEADS * HEAD_P
    o3, o4 = 2 * W + SI, 2 * W + SI + CD
    win = jnp.concatenate([w_in[:, :, o3:o4], w_in[:, :, 0:o3],
                           jnp.pad(w_in[:, :, o4:], ((0, 0), (0, 0), (0, LANES - HEADS)))], axis=2).astype(bf16)
    hd = W // HEADS
    per = 256 // hd
    nblk = HEADS // per

    def blockdiag(wh):
        wh = wh.reshape(depth, nblk, per, hd, hd)
        eye = jnp.eye(per, dtype=wh.dtype)
        return (eye[None, None, :, None, :, None] * wh[:, :, :, :, None, :]).reshape(depth, nblk, per * hd, per * hd)

    wg = jnp.concatenate([blockdiag(lru_wa), blockdiag(lru_wx)], axis=3).astype(bf16)
    row = lambda v: v.reshape(depth, 1, -1).astype(f32)
    pad_h = lambda v: jnp.pad(v.astype(f32), ((0, 0), (0, LANES - HEADS))).reshape(depth, 1, LANES)
    return dict(
        gmix=row(norm_mix), win=win, lcw=lru_conv_w, lcb=row(lru_conv_b), wg=wg,
        bg=jnp.stack([lru_ba, lru_bx], axis=1), ap=row(lru_a_param), lnorm=row(lru_out_norm),
        scw=ssd_conv_w, scb=row(ssd_conv_b), dtb=pad_h(ssd_dt_bias), alog=pad_h(ssd_a_log),
        dvec=row(jnp.repeat(ssd_d, HEAD_P, axis=1)), snorm=row(ssd_out_norm), wout=w_out.astype(bf16),
        gffn=row(norm_ffn), wup=ffn_w_up.astype(bf16), cw=ffn_conv_w, cb=row(ffn_conv_b),
        wdn=ffn_w_down.astype(bf16))


def _tile(S, cap):
    T = min(S, cap)
    while S % T:
        T -= CHUNK
    return T


def kernel(x_prompt, x_sample, state_lru_conv, state_lru_h, state_ssd_conv, state_ssd, state_ffn_conv, meta_tokens, norm_mix, w_in, lru_conv_w, lru_conv_b, lru_wa, lru_ba, lru_wx, lru_bx, lru_a_param, lru_out_norm, ssd_conv_w, ssd_conv_b, ssd_dt_bias, ssd_a_log, ssd_d, ssd_out_norm, w_out, norm_ffn, ffn_w_up, ffn_conv_w, ffn_conv_b, ffn_w_down, norm_final):
    B, S, D = x_prompt.shape
    Bs, Ls, _ = x_sample.shape
    depth = w_in.shape[0]
    n_meta = meta_tokens.shape[0]
    W = lru_conv_w.shape[2]
    CD = ssd_conv_w.shape[2]
    SI = HEADS * HEAD_P
    F2 = ffn_w_up.shape[2]
    w = _prep_weights(norm_mix, w_in, lru_conv_w, lru_conv_b, lru_wa, lru_ba, lru_wx, lru_bx, lru_a_param,
                      lru_out_norm, ssd_conv_w, ssd_conv_b, ssd_dt_bias, ssd_a_log, ssd_d, ssd_out_norm, w_out,
                      norm_ffn, ffn_w_up, ffn_conv_w, ffn_conv_b, ffn_w_down)
    gfin = norm_final.reshape(1, D).astype(f32)

    xm = jnp.broadcast_to(meta_tokens.astype(f32)[:, None, :], (n_meta, B, D)).reshape(n_meta * B, D)
    zero = dict(lc=jnp.zeros((3 * B, W), f32), lh=jnp.zeros((B, W), f32), sc=jnp.zeros((3 * B, CD), f32),
                fc=jnp.zeros((2 * B, F2), f32))
    zero_ss = jnp.zeros((1, B, SI, SSD_N), f32)
    meta_st, meta_ss = [], None
    for l in range(depth):
        xm, new, meta_ss = _short_layer(xm, w, l, zero, zero_ss, 0, meta_ss, gfin, L=n_meta, Bn=B, final=False)
        meta_st.append(new)

    Tm = _tile(S, 256)
    TTf = 512 // B if S % (512 // B) == 0 else S
    xp = x_prompt
    p_lc, p_lh, p_sc, p_ss, p_fc = [], [], [], [], []
    for l in range(depth):
        m = meta_st[l]
        st = dict(lc=_from_tm(m["lc"], B), lh=m["lh"].reshape(B, 1, W), sc=_from_tm(m["sc"], B), ss=meta_ss[l])
        xp, lc, lh, sc, ss = _p_mixer(xp, w, l, st, T=Tm, NB=2)
        xp, fc = _p_ffn(xp, w, l, m["fc"], gfin, TT=TTf, FC=3072, final=(l == depth - 1))
        p_lc.append(lc); p_lh.append(lh.reshape(B, W)); p_sc.append(sc)
        p_ss.append(ss.reshape(B, HEADS, HEAD_P, SSD_N)); p_fc.append(_from_tm(fc, B))

    xs = _to_tm(x_sample)
    ss_in = state_ssd.reshape(depth, Bs, SI, SSD_N)
    s_ss = None
    s_lc, s_lh, s_sc, s_fc = [], [], [], []
    for l in range(depth):
        st = dict(lc=_to_tm(state_lru_conv[l]), lh=state_lru_h[l], sc=_to_tm(state_ssd_conv[l]),
                  fc=_to_tm(state_ffn_conv[l]))
        xs, new, s_ss = _short_layer(xs, w, l, st, ss_in, l, s_ss, gfin, L=Ls, Bn=Bs, final=(l == depth - 1))
        s_lc.append(_from_tm(new["lc"], Bs)); s_lh.append(new["lh"]); s_sc.append(_from_tm(new["sc"], Bs))
        s_fc.append(_from_tm(new["fc"], Bs))

    st = jnp.stack
    return (xp, _from_tm(xs, Bs), st(p_lc), st(p_lh), st(p_sc), st(p_ss), st(p_fc),
            st(s_lc), st(s_lh), st(s_sc), s_ss.reshape(depth, Bs, HEADS, HEAD_P, SSD_N), st(s_fc))
```

```python
import functools

import jax
import jax.numpy as jnp
from jax import lax
from jax.experimental import pallas as pl
from jax.experimental.pallas import tpu as pltpu

f32 = jnp.float32
bf16 = jnp.bfloat16

EPS = 1e-6
LRU_C = 8.0
HEADS = 8
HEAD_P = 64
SSD_N = 128
CHUNK = 128
LANES = 128
SUBLANES = 8
VMEM_LIMIT = 56 * 1024 * 1024
HI = lax.Precision.HIGHEST


def _rms(x, g):
    return x * lax.rsqrt(jnp.mean(x * x, axis=-1, keepdims=True) + EPS) * g


def _silu(x):
    return x * jax.nn.sigmoid(x)


def _softplus(x):
    return jnp.maximum(x, 0.0) + jnp.log1p(jnp.exp(-jnp.abs(x)))


def _dot(a, b):
    return jnp.dot(a, b, preferred_element_type=f32)


def _lru_coeffs(xc, pre_r, pre_i, a_param):
    r = jax.nn.sigmoid(pre_r)
    i = jax.nn.sigmoid(pre_i)
    log_a = -LRU_C * r * _softplus(-a_param)
    a = jnp.exp(log_a)
    th = jnp.tanh(log_a)
    mult = jnp.sqrt(-2.0 * th / (1.0 - th))
    return a, mult * (i * xc)


def _gates(xc, wg_ref, bg_ref):
    half = wg_ref.shape[1]
    pr, pi = [], []
    for j in range(wg_ref.shape[0]):
        ri = _dot(xc[:, j * half:(j + 1) * half].astype(bf16), wg_ref[j])
        pr.append(ri[:, :half])
        pi.append(ri[:, half:])
    bg = bg_ref[...]
    return jnp.concatenate(pr, axis=1) + bg[0:1], jnp.concatenate(pi, axis=1) + bg[1:2]


def _head_expand():
    r = lax.broadcasted_iota(jnp.int32, (LANES, HEADS * HEAD_P), 0)
    c = lax.broadcasted_iota(jnp.int32, (LANES, HEADS * HEAD_P), 1)
    return (r == lax.shift_right_logical(c, 6)).astype(f32)


def _layer_spec(a, l):
    nd = a.ndim - 1
    return pl.BlockSpec((None,) + a.shape[1:], lambda *_: (l,) + (0,) * nd, pipeline_mode=pl.Buffered(1))


def _full_spec(a):
    nd = a.ndim
    return pl.BlockSpec(a.shape, lambda *_: (0,) * nd)


MIX_KEYS = ("gmix", "win", "lcw", "lcb", "wg", "bg", "ap", "lnorm", "scw", "scb", "dtb", "alog", "dvec")


def _scan_rows(a, b, h0):
    T, C = a.shape
    G = T // SUBLANES
    a3 = a.reshape(G, SUBLANES, C)
    b3 = b.reshape(G, SUBLANES, C)
    sub = lax.broadcasted_iota(jnp.int32, a3.shape, 1)
    for d in (1, 2, 4):
        ar = pltpu.roll(a3, d, axis=1)
        br = pltpu.roll(b3, d, axis=1)
        m = sub >= d
        b3 = jnp.where(m, a3 * br + b3, b3)
        a3 = jnp.where(m, a3 * ar, a3)
    hs = []
    hp = h0
    for gi in range(G):
        hb = a3[gi] * hp + b3[gi]
        hs.append(hb)
        hp = hb[SUBLANES - 1:SUBLANES, :]
    return jnp.concatenate(hs, axis=0), hp


def _p_mixer_kernel(x_ref, xn_ref, gmix_ref, win_ref, lcw_ref, lcb_ref, wg_ref, bg_ref, ap_ref, lnorm_ref,
                    scw_ref, scb_ref, dtb_ref, alog_ref, dvec_ref, snorm_ref, wout_ref,
                    st_lc_ref, st_lh_ref, st_sc_ref, st_ss_ref,
                    o_ref, n_lc_ref, n_lh_ref, n_sc_ref, n_ss_ref,
                    ext_l, ext_s, hcar, s_pair, ys_sc, xb_sc, dt_sc, gz_sc, mix_sc, hn_sc, *, T, NB):
    t = pl.program_id(1)
    nt = pl.num_programs(1)
    W = ext_l.shape[2]
    CD = ext_s.shape[2]
    SI = HEADS * HEAD_P
    NP = HEADS // 2

    @pl.when(t == 0)
    def _():
        for s in range(NB):
            ext_l[s, 5:8, :] = st_lc_ref[s]
            ext_s[s, 5:8, :] = st_sc_ref[s]
            hcar[s, 0:1, :] = st_lh_ref[s]
            for j in range(NP):
                s_pair[s, j] = st_ss_ref[s, j * LANES:(j + 1) * LANES, :].T

    a_row = -jnp.exp(alog_ref[...])
    row = lax.broadcasted_iota(jnp.int32, (CHUNK, CHUNK), 0)
    col = lax.broadcasted_iota(jnp.int32, (CHUNK, CHUNK), 1)
    causal = col <= row
    ltri = causal.astype(f32)
    lane_l = col < HEAD_P
    eexp = _head_expand()
    dvec = dvec_ref[...]
    lcw = lcw_ref[...]
    scw = scw_ref[...]

    def stage_a(s, src_ref, i):
        def norm():
            hn_sc[s] = _rms(src_ref[i], gmix_ref[...]).astype(bf16)

        def proj(dst, d0, c0, n):
            def step():
                dst[s, d0[0]:d0[0] + T, d0[1]:d0[1] + n] = _dot(hn_sc[s], win_ref[:, c0:c0 + n])
            return step

        def dt():
            dt_sc[s] = _softplus(_dot(hn_sc[s], win_ref[:, CD + 3 * W:CD + 3 * W + LANES]) + dtb_ref[...])

        steps = [norm, proj(ext_l, (8, 0), CD, W)]
        steps += [proj(ext_s, (8, c), c, W) for c in range(0, CD, W)]
        steps += [proj(gz_sc, (0, 0), CD + W, W), proj(gz_sc, (0, W), CD + 2 * W, W), dt]
        return steps

    def stage_b(s):
        xc = lcb_ref[...] + lcw[3:4] * ext_l[s, 8:T + 8, :]
        for k in range(3):
            xc = xc + lcw[k:k + 1] * ext_l[s, 5 + k:5 + k + T, :]
        ext_l[s, 5:8, :] = ext_l[s, T + 5:T + 8, :]
        pre_r, pre_i = _gates(xc, wg_ref, bg_ref)
        a, bt = _lru_coeffs(xc, pre_r, pre_i, ap_ref[...])
        yield
        hl, hlast = _scan_rows(a, bt, hcar[s, 0:1, :])
        hcar[s, 0:1, :] = hlast
        yield
        mix_sc[s, :, 0:W] = _rms(hl * jax.nn.gelu(gz_sc[s, :, 0:W]), lnorm_ref[...]).astype(bf16)
        yield

        xb = scb_ref[...] + scw[3:4] * ext_s[s, 8:T + 8, :]
        for k in range(3):
            xb = xb + scw[k:k + 1] * ext_s[s, 5 + k:5 + k + T, :]
        ext_s[s, 5:8, :] = ext_s[s, T + 5:T + 8, :]
        xb_sc[s] = _silu(xb)
        yield

        for q in range(T // CHUNK):
            r0 = q * CHUNK
            xs_c = xb_sc[s, r0:r0 + CHUNK, 0:SI]
            dt_c = dt_sc[s, r0:r0 + CHUNK, :]
            cs_col = jnp.dot(ltri, dt_c * a_row, precision=HI, preferred_element_type=f32)
            cs_row = cs_col.T[0:HEADS]
            dt_row = dt_c.T[0:HEADS]
            ecol = jnp.exp(cs_col)
            w8 = jnp.exp(cs_row[:, CHUNK - 1:CHUNK] - cs_row) * dt_row
            dec_x = jnp.exp(jnp.dot(cs_col[CHUNK - SUBLANES:CHUNK], eexp, precision=HI,
                                    preferred_element_type=f32)[SUBLANES - 1:SUBLANES])
            for g in range(2):
                bm_t = xb_sc[s, r0:r0 + CHUNK, SI + g * SSD_N:SI + (g + 1) * SSD_N].T
                cm = xb_sc[s, r0:r0 + CHUNK, SI + 2 * SSD_N + g * SSD_N:SI + 2 * SSD_N + (g + 1) * SSD_N]
                scores = _dot(cm.astype(bf16), bm_t.astype(bf16))
                for jj in range(2):
                    j = 2 * g + jj
                    lhs, lhs2 = [], []
                    for h in (2 * j, 2 * j + 1):
                        lmat = jnp.where(causal, jnp.exp(cs_col[:, h:h + 1] - cs_row[h:h + 1]), 0.0)
                        lhs.append(scores * lmat * dt_row[h:h + 1])
                        lhs.append(cm * ecol[:, h:h + 1])
                        lhs2.append(bm_t * w8[h:h + 1])
                    xp = xs_c[:, j * LANES:(j + 1) * LANES]
                    sp = s_pair[s, j]
                    xl = jnp.where(lane_l, xp, 0.0)
                    xr = jnp.where(lane_l, 0.0, xp)
                    rhs = jnp.concatenate([xl, jnp.where(lane_l, sp, 0.0), xr, jnp.where(lane_l, 0.0, sp)], axis=0)
                    y = _dot(jnp.concatenate(lhs, axis=1).astype(bf16), rhs.astype(bf16))
                    upd = _dot(jnp.concatenate(lhs2, axis=1).astype(bf16),
                               jnp.concatenate([xl, xr], axis=0).astype(bf16))
                    s_pair[s, j] = dec_x[:, j * LANES:(j + 1) * LANES] * sp + upd
                    ys_sc[s, r0:r0 + CHUNK, j * LANES:(j + 1) * LANES] = y + xp * dvec[:, j * LANES:(j + 1) * LANES]
                yield

        mix_sc[s, :, W:W + SI] = _rms(ys_sc[s] * _silu(gz_sc[s, :, W:2 * W]), snorm_ref[...]).astype(bf16)

    def stage_c(s):
        o_ref[s] = x_ref[s] + _dot(mix_sc[s], wout_ref[...])

    @pl.when((pl.program_id(0) == 0) & (t == 0))
    def _():
        for step in stage_a(0, x_ref, 0):
            step()

    for s in range(NB):
        pending = stage_a(s + 1, x_ref, s + 1) if s + 1 < NB else stage_a(0, xn_ref, 0)
        for _ in stage_b(s):
            if pending:
                pending.pop(0)()
        for step in pending:
            step()
        stage_c(s)

    @pl.when(t == nt - 1)
    def _():
        for s in range(NB):
            n_lc_ref[s] = ext_l[s, 5:8, :]
            n_sc_ref[s] = ext_s[s, 5:8, :]
            n_lh_ref[s] = hcar[s, 0:1, :]
            for j in range(NP):
                n_ss_ref[s, j * LANES:(j + 1) * LANES, :] = s_pair[s, j].T


def _p_mixer(x, w, l, st, *, T, NB):
    B, S, D = x.shape
    W = w["lcw"].shape[-1]
    CD = w["scw"].shape[-1]
    SI = HEADS * HEAD_P
    consts = [w[k] for k in MIX_KEYS] + [w["snorm"], w["wout"]]
    st_shapes = [(NB, 3, W), (NB, 1, W), (NB, 3, CD), (NB, SI, SSD_N)]
    st_in_specs = [pl.BlockSpec((None,) + s, lambda b, t: (l, b, 0, 0)) for s in st_shapes]
    st_specs = [pl.BlockSpec(s, lambda b, t: (b, 0, 0)) for s in st_shapes]
    x_spec = pl.BlockSpec((NB, T, D), lambda b, t: (b, t, 0))
    nb, nt = B // NB, S // T

    def next_first(b, t):
        wrap = t + 1 == nt
        return jnp.minimum(jnp.where(wrap, b + 1, b), nb - 1) * NB, jnp.where(wrap, 0, t + 1), 0

    return pl.pallas_call(
        functools.partial(_p_mixer_kernel, T=T, NB=NB),
        grid=(nb, nt),
        in_specs=[x_spec, pl.BlockSpec((1, T, D), next_first)] + [_layer_spec(c, l) for c in consts] + st_in_specs,
        out_specs=[x_spec] + st_specs,
        out_shape=[jax.ShapeDtypeStruct((B, S, D), f32),
                   jax.ShapeDtypeStruct((B, 3, W), f32), jax.ShapeDtypeStruct((B, 1, W), f32),
                   jax.ShapeDtypeStruct((B, 3, CD), f32), jax.ShapeDtypeStruct((B, SI, SSD_N), f32)],
        scratch_shapes=[pltpu.VMEM((NB, T + 8, W), f32), pltpu.VMEM((NB, T + 8, CD), f32),
                        pltpu.VMEM((NB, SUBLANES, W), f32), pltpu.VMEM((NB, HEADS // 2, SSD_N, LANES), f32),
                        pltpu.VMEM((NB, T, SI), f32), pltpu.VMEM((NB, T, CD), f32), pltpu.VMEM((NB, T, LANES), f32),
                        pltpu.VMEM((NB, T, 2 * W), f32), pltpu.VMEM((NB, T, W + SI), bf16),
                        pltpu.VMEM((NB, T, D), bf16)],
        compiler_params=pltpu.CompilerParams(dimension_semantics=("arbitrary", "arbitrary"),
                                             vmem_limit_bytes=VMEM_LIMIT),
        name="p_mixer",
    )(x, x, *consts, st["lc"], st["lh"], st["sc"], st["ss"])


def _p_ffn_kernel(x_ref, g_ref, wup_ref, cw_ref, cb_ref, wdn_ref, gfin_ref, st_ref,
                  o_ref, nst_ref, carry, *, TT, final):
    i = pl.program_id(0)
    B, _, D = x_ref.shape
    R = TT * B
    F = wdn_ref.shape[0]

    @pl.when(i == 0)
    def _():
        carry[...] = st_ref[...]

    x = jnp.swapaxes(x_ref[...], 0, 1).reshape(R, D)
    hn = _rms(x, g_ref[...]).astype(bf16)
    ys = []
    for half in range(2):
        c0 = half * F
        u = _dot(hn, wup_ref[:, c0:c0 + F])
        ext = jnp.concatenate([carry[:, c0:c0 + F], u], axis=0)
        cw = cw_ref[:, c0:c0 + F]
        ys.append(cb_ref[:, c0:c0 + F] + cw[0:1] * ext[0:R] + cw[1:2] * ext[B:B + R] + cw[2:3] * u)
        carry[:, c0:c0 + F] = ext[R:R + 2 * B]
    act = (jax.nn.gelu(ys[0]) * ys[1]).astype(bf16)
    out = x + _dot(act, wdn_ref[...])
    if final:
        out = _rms(out, gfin_ref[...])
    o_ref[...] = jnp.swapaxes(out.reshape(TT, B, D), 0, 1)

    @pl.when(i == pl.num_programs(0) - 1)
    def _():
        nst_ref[...] = carry[...]


def _p_ffn(x, w, l, st_fc, gfin, *, TT, final):
    B, S, D = x.shape
    F2 = w["wup"].shape[-1]
    assert B % SUBLANES == 0, "time-major row tiles need whole sublane groups per time step"
    consts = [w["gffn"], w["wup"], w["cw"], w["cb"], w["wdn"]]
    x_spec = pl.BlockSpec((B, TT, D), lambda i: (0, i, 0))
    return pl.pallas_call(
        functools.partial(_p_ffn_kernel, TT=TT, final=final),
        grid=(S // TT,),
        in_specs=[x_spec] + [_layer_spec(c, l) for c in consts] + [_full_spec(gfin), _layer_spec(st_fc, l)],
        out_specs=[x_spec, pl.BlockSpec((2 * B, F2), lambda i: (0, 0))],
        out_shape=[jax.ShapeDtypeStruct((B, S, D), f32), jax.ShapeDtypeStruct((2 * B, F2), f32)],
        scratch_shapes=[pltpu.VMEM((2 * B, F2), f32)],
        compiler_params=pltpu.CompilerParams(dimension_semantics=("arbitrary",), vmem_limit_bytes=VMEM_LIMIT),
        name="p_ffn",
    )(x, *consts, gfin, st_fc)


def _tm_conv(state, u, w, b, n_rows, L):
    K = w.shape[0]
    ext = jnp.concatenate([state, u], axis=0)
    R = L * n_rows
    y = b
    for k in range(K):
        y = y + w[k:k + 1] * ext[k * n_rows:k * n_rows + R]
    return y, ext[R:R + (K - 1) * n_rows]


def _s_mix_a_kernel(x_ref, gmix_ref, win_ref, lcw_ref, lcb_ref, wg_ref, bg_ref, ap_ref, lnorm_ref,
                    scw_ref, scb_ref, dtb_ref, alog_ref, dvec_ref,
                    st_lc_ref, st_lh_ref, st_sc_ref,
                    lru_ref, z_ref, bc_ref, ypart_ref, ecs_ref, xw_ref, dec_ref,
                    n_lc_ref, n_lh_ref, n_sc_ref, *, L, Bn):
    W = lcw_ref.shape[1]
    CD = scw_ref.shape[1]
    SI = HEADS * HEAD_P
    hn = _rms(x_ref[...], gmix_ref[...]).astype(bf16)

    xc, n_lc = _tm_conv(st_lc_ref[...], _dot(hn, win_ref[:, CD:CD + W]), lcw_ref[...], lcb_ref[...], Bn, L)
    n_lc_ref[...] = n_lc
    pre_r, pre_i = _gates(xc, wg_ref, bg_ref)
    a, bt = _lru_coeffs(xc, pre_r, pre_i, ap_ref[...])
    hp = st_lh_ref[...]
    hs = []
    for t in range(L):
        hp = a[t * Bn:(t + 1) * Bn] * hp + bt[t * Bn:(t + 1) * Bn]
        hs.append(hp)
    n_lh_ref[...] = hp
    gate = _dot(hn, win_ref[:, CD + W:CD + 2 * W])
    lru_ref[...] = _rms(jnp.concatenate(hs, axis=0) * jax.nn.gelu(gate), lnorm_ref[...])
    z_ref[...] = _dot(hn, win_ref[:, CD + 2 * W:CD + 3 * W])

    xb, n_sc = _tm_conv(st_sc_ref[...], _dot(hn, win_ref[:, 0:CD]), scw_ref[...], scb_ref[...], Bn, L)
    n_sc_ref[...] = n_sc
    xb = _silu(xb)
    xs = xb[:, 0:SI]
    R = L * Bn
    bc_ref[0:R, :] = xb[:, SI:CD]
    if bc_ref.shape[0] > R:
        bc_ref[R:, :] = jnp.zeros((bc_ref.shape[0] - R, CD - SI), f32)
        xw_ref[R:, :] = jnp.zeros((xw_ref.shape[0] - R, SI), f32)
    dt = _softplus(_dot(hn, win_ref[:, CD + 3 * W:CD + 3 * W + LANES]) + dtb_ref[...])
    dA = dt * (-jnp.exp(alog_ref[...]))
    css = []
    cs = jnp.zeros((Bn, LANES), f32)
    for t in range(L):
        cs = cs + dA[t * Bn:(t + 1) * Bn]
        css.append(cs)
    dec_ref[...] = jnp.exp(cs)
    eexp = _head_expand()
    csx = jnp.dot(jnp.concatenate(css, axis=0), eexp, precision=HI, preferred_element_type=f32)
    dtx = jnp.dot(dt, eexp, precision=HI, preferred_element_type=f32)
    ecs_ref[...] = jnp.exp(csx)
    cs_end = csx[(L - 1) * Bn:L * Bn]
    dvec = dvec_ref[...]
    half = SI // 2
    for t in range(L):
        rt = slice(t * Bn, (t + 1) * Bn)
        xw_ref[rt, :] = xs[rt] * jnp.exp(cs_end - csx[rt]) * dtx[rt]
        acc = dvec * xs[rt]
        cm = xb[rt, SI + 2 * SSD_N:CD]
        for s in range(t + 1):
            rs = slice(s * Bn, (s + 1) * Bn)
            prod = cm * xb[rs, SI:SI + 2 * SSD_N]
            gx = jnp.concatenate(
                [jnp.broadcast_to(jnp.sum(prod[:, 0:SSD_N], axis=-1, keepdims=True), (Bn, half)),
                 jnp.broadcast_to(jnp.sum(prod[:, SSD_N:2 * SSD_N], axis=-1, keepdims=True), (Bn, half))], axis=1)
            acc = acc + gx * jnp.exp(csx[rt] - csx[rs]) * dtx[rs] * xs[rs]
        ypart_ref[rt, :] = acc


def _s_mix_a(x, w, l, st, ls, *, L, Bn):
    R, D = x.shape
    W = w["lcw"].shape[-1]
    CD = w["scw"].shape[-1]
    SI = HEADS * HEAD_P
    Rp = -(-L // SUBLANES) * SUBLANES * Bn
    sds = jax.ShapeDtypeStruct
    consts = [w[k] for k in MIX_KEYS]
    states = [st["lc"], st["lh"], st["sc"]]
    out_shape = [sds((R, W), f32), sds((R, SI), f32), sds((Rp, CD - SI), f32), sds((R, SI), f32),
                 sds((R, SI), f32), sds((Rp, SI), f32), sds((Bn, LANES), f32),
                 sds((3 * Bn, W), f32), sds((Bn, W), f32), sds((3 * Bn, CD), f32)]
    return pl.pallas_call(
        functools.partial(_s_mix_a_kernel, L=L, Bn=Bn),
        grid=(1,),
        in_specs=[_full_spec(x)] + [_layer_spec(c, l) for c in consts] + [_layer_spec(a, ls) for a in states],
        out_specs=[_full_spec(o) for o in out_shape],
        out_shape=out_shape,
        compiler_params=pltpu.CompilerParams(dimension_semantics=("arbitrary",), vmem_limit_bytes=VMEM_LIMIT),
        name="s_mix_a",
    )(x, *consts, *states)


def _s_ssd_kernel(dec_ref, xw_ref, bc_ref, st_ref, _, yraw_ref, nst_ref, *, bb):
    i = pl.program_id(0)
    GW = (HEADS // 2) * HEAD_P
    xw_all = jnp.swapaxes(xw_ref[...], 0, 1).astype(bf16)
    bc_all = jnp.swapaxes(bc_ref[...], 0, 1).astype(bf16)
    yraw = []
    for j in range(bb):
        xw = xw_all[j]
        bc = bc_all[j]
        ys = []
        for g in range(2):
            sg = st_ref[j, g * GW:(g + 1) * GW, :]
            upd = lax.dot_general(xw[:, g * GW:(g + 1) * GW], bc[:, g * SSD_N:(g + 1) * SSD_N],
                                  (((0,), (0,)), ((), ())), preferred_element_type=f32)
            cg = bc[:, 2 * SSD_N + g * SSD_N:2 * SSD_N + (g + 1) * SSD_N]
            ys.append(lax.dot_general(cg, sg.astype(bf16), (((1,), (1,)), ((), ())),
                                      preferred_element_type=f32))
            for hh in range(HEADS // 2):
                h = g * (HEADS // 2) + hh
                r0 = g * GW + hh * HEAD_P
                nst_ref[j, r0:r0 + HEAD_P, :] = (dec_ref[(i * bb + j) * HEADS + h] * sg[hh * HEAD_P:(hh + 1) * HEAD_P, :]
                                                 + upd[hh * HEAD_P:(hh + 1) * HEAD_P, :])
        yraw.append(jnp.concatenate(ys, axis=1))
    yraw_ref[...] = jnp.swapaxes(jnp.stack(yraw), 0, 1)


def _s_ssd(dec, xw, bc, st_all, l_in, out_all, l_out, *, bb):
    Lp, Bn, SI = xw.shape
    blk3 = lambda c: pl.BlockSpec((Lp, bb, c), lambda i: (0, i, 0))
    st_in = pl.BlockSpec((None, bb, SI, SSD_N), lambda i: (l_in, i, 0, 0))
    st_out = pl.BlockSpec((None, bb, SI, SSD_N), lambda i: (l_out, i, 0, 0))
    return pl.pallas_call(
        functools.partial(_s_ssd_kernel, bb=bb),
        grid=(Bn // bb,),
        in_specs=[pl.BlockSpec(memory_space=pltpu.SMEM), blk3(SI), blk3(bc.shape[2]), st_in,
                  pl.BlockSpec(memory_space=pl.ANY)],
        out_specs=[blk3(SI), st_out],
        out_shape=[jax.ShapeDtypeStruct((Lp, Bn, SI), f32), jax.ShapeDtypeStruct(out_all.shape, f32)],
        input_output_aliases={4: 1},
        compiler_params=pltpu.CompilerParams(dimension_semantics=("arbitrary",), vmem_limit_bytes=VMEM_LIMIT),
        name="s_ssd",
    )(dec, xw, bc, st_all, out_all)


def _s_mix_c_kernel(x_ref, lru_ref, ypart_ref, yraw_ref, ecs_ref, z_ref, snorm_ref, wout_ref, o_ref):
    W = lru_ref.shape[1]
    SI = ypart_ref.shape[1]
    ys = ypart_ref[...] + yraw_ref[...] * ecs_ref[...]
    ssd_out = _rms(ys * _silu(z_ref[...]), snorm_ref[...]).astype(bf16)
    o_ref[...] = (x_ref[...] + _dot(lru_ref[...].astype(bf16), wout_ref[0:W, :])
                  + _dot(ssd_out, wout_ref[W:W + SI, :]))


def _s_mix_c(x, lru, ypart, yraw, ecs, z, w, l):
    acts = [x, lru, ypart, yraw, ecs, z]
    lead = lambda a: pl.BlockSpec((x.shape[0], a.shape[1]), lambda *_: (0, 0))
    return pl.pallas_call(
        _s_mix_c_kernel,
        grid=(1,),
        in_specs=[lead(a) for a in acts] + [_layer_spec(w["snorm"], l), _layer_spec(w["wout"], l)],
        out_specs=_full_spec(x),
        out_shape=jax.ShapeDtypeStruct(x.shape, f32),
        compiler_params=pltpu.CompilerParams(dimension_semantics=("arbitrary",), vmem_limit_bytes=VMEM_LIMIT),
        name="s_mix_c",
    )(*acts, w["snorm"], w["wout"])


def _s_ffn_kernel(x_ref, g_ref, wug_ref, wuv_ref, cwg_ref, cwv_ref, cbg_ref, cbv_ref, wdn_ref, gfin_ref,
                  stg_ref, stv_ref, o_ref, nstg_ref, nstv_ref, hn_sc, acc_sc, *, L, Bn, final):
    c = pl.program_id(0)

    @pl.when(c == 0)
    def _():
        hn_sc[...] = _rms(x_ref[...], g_ref[...]).astype(bf16)
        acc_sc[...] = jnp.zeros(acc_sc.shape, f32)

    hn = hn_sc[...]
    yg, nstg = _tm_conv(stg_ref[...], _dot(hn, wug_ref[...]), cwg_ref[...], cbg_ref[...], Bn, L)
    yv, nstv = _tm_conv(stv_ref[...], _dot(hn, wuv_ref[...]), cwv_ref[...], cbv_ref[...], Bn, L)
    nstg_ref[...] = nstg
    nstv_ref[...] = nstv
    acc_sc[...] += _dot((jax.nn.gelu(yg) * yv).astype(bf16), wdn_ref[...])

    @pl.when(c == pl.num_programs(0) - 1)
    def _():
        out = x_ref[...] + acc_sc[...]
        if final:
            out = _rms(out, gfin_ref[...])
        o_ref[...] = out


def _s_ffn(x, w, l, st_fc, ls, gfin, *, L, Bn, FC, final):
    R, D = x.shape
    F = w["wdn"].shape[1]
    NC = F // FC
    lcolg = lambda rows, k=l: pl.BlockSpec((None, rows, FC), lambda c: (k, 0, c))
    lcolv = lambda rows, k=l: pl.BlockSpec((None, rows, FC), lambda c: (k, 0, c + NC))
    colg = lambda rows: pl.BlockSpec((rows, FC), lambda c: (0, c))
    return pl.pallas_call(
        functools.partial(_s_ffn_kernel, L=L, Bn=Bn, final=final),
        grid=(NC,),
        in_specs=[_full_spec(x), pl.BlockSpec((None, 1, D), lambda c: (l, 0, 0)),
                  lcolg(D), lcolv(D), lcolg(3), lcolv(3), lcolg(1), lcolv(1),
                  pl.BlockSpec((None, FC, D), lambda c: (l, c, 0)), _full_spec(gfin),
                  lcolg(2 * Bn, ls), lcolv(2 * Bn, ls)],
        out_specs=[_full_spec(x), colg(2 * Bn), colg(2 * Bn)],
        out_shape=[jax.ShapeDtypeStruct((R, D), f32), jax.ShapeDtypeStruct((2 * Bn, F), f32),
                   jax.ShapeDtypeStruct((2 * Bn, F), f32)],
        scratch_shapes=[pltpu.VMEM((R, D), bf16), pltpu.VMEM((R, D), f32)],
        compiler_params=pltpu.CompilerParams(dimension_semantics=("arbitrary",), vmem_limit_bytes=VMEM_LIMIT),
        name="s_ffn",
    )(x, w["gffn"], w["wup"], w["wup"], w["cw"], w["cw"], w["cb"], w["cb"], w["wdn"], gfin, st_fc, st_fc)


def _to_tm(a):
    *lead, Bn, K, C = a.shape
    return jnp.swapaxes(a, -3, -2).reshape(*lead, K * Bn, C)


def _from_tm(a, Bn):
    *lead, R, C = a.shape
    return jnp.swapaxes(a.reshape(*lead, R // Bn, Bn, C), -3, -2)


def _short_layer(x, w, l, st, ls, ss_out, gfin, *, L, Bn, final):
    (lru, z, bc, ypart, ecs, xw, dec, n_lc, n_lh, n_sc) = _s_mix_a(x, w, l, st, ls, L=L, Bn=Bn)
    tm3 = lambda a: a.reshape(a.shape[0] // Bn, Bn, a.shape[1])
    yraw, ss_out = _s_ssd(dec[:, :HEADS].reshape(-1), tm3(xw), tm3(bc), st["ss"], ls, ss_out, l, bb=SUBLANES)
    x = _s_mix_c(x, lru, ypart, yraw.reshape(-1, yraw.shape[2]), ecs, z, w, l)
    x, nfg, nfv = _s_ffn(x, w, l, st["fc"], ls, gfin, L=L, Bn=Bn, FC=512, final=final)
    new = dict(lc=n_lc, lh=n_lh, sc=n_sc, fc=jnp.concatenate([nfg, nfv], axis=1))
    return x, new, ss_out


def _prep_weights(norm_mix, w_in, lru_conv_w, lru_conv_b, lru_wa, lru_ba, lru_wx, lru_bx, lru_a_param,
                  lru_out_norm, ssd_conv_w, ssd_conv_b, ssd_dt_bias, ssd_a_log, ssd_d, ssd_out_norm, w_out,
                  norm_ffn, ffn_w_up, ffn_conv_w, ffn_conv_b, ffn_w_down):
    depth = w_in.shape[0]
    W = lru_conv_w.shape[2]
    CD = ssd_conv_w.shape[2]
    SI = HEADS * HEAD_P
    o3, o4 = 2 * W + SI, 2 * W + SI + CD
    win = jnp.concatenate([w_in[:, :, o3:o4], w_in[:, :, 0:o3],
                           jnp.pad(w_in[:, :, o4:], ((0, 0), (0, 0), (0, LANES - HEADS)))], axis=2).astype(bf16)
    hd = W // HEADS
    per = 256 // hd
    nblk = HEADS // per

    def blockdiag(wh):
        wh = wh.reshape(depth, nblk, per, hd, hd)
        eye = jnp.eye(per, dtype=wh.dtype)
        return (eye[None, None, :, None, :, None] * wh[:, :, :, :, None, :]).reshape(depth, nblk, per * hd, per * hd)

    wg = jnp.concatenate([blockdiag(lru_wa), blockdiag(lru_wx)], axis=3).astype(bf16)
    row = lambda v: v.reshape(depth, 1, -1).astype(f32)
    pad_h = lambda v: jnp.pad(v.astype(f32), ((0, 0), (0, LANES - HEADS))).reshape(depth, 1, LANES)
    return dict(
        gmix=row(norm_mix), win=win, lcw=lru_conv_w, lcb=row(lru_conv_b), wg=wg,
        bg=jnp.stack([lru_ba, lru_bx], axis=1), ap=row(lru_a_param), lnorm=row(lru_out_norm),
        scw=ssd_conv_w, scb=row(ssd_conv_b), dtb=pad_h(ssd_dt_bias), alog=pad_h(ssd_a_log),
        dvec=row(jnp.repeat(ssd_d, HEAD_P, axis=1)), snorm=row(ssd_out_norm), wout=w_out.astype(bf16),
        gffn=row(norm_ffn), wup=ffn_w_up.astype(bf16), cw=ffn_conv_w, cb=row(ffn_conv_b),
        wdn=ffn_w_down.astype(bf16))


def _tile(S, cap):
    T = min(S, cap)
    while S % T:
        T -= CHUNK
    return T


def kernel(x_prompt, x_sample, state_lru_conv, state_lru_h, state_ssd_conv, state_ssd, state_ffn_conv, meta_tokens, norm_mix, w_in, lru_conv_w, lru_conv_b, lru_wa, lru_ba, lru_wx, lru_bx, lru_a_param, lru_out_norm, ssd_conv_w, ssd_conv_b, ssd_dt_bias, ssd_a_log, ssd_d, ssd_out_norm, w_out, norm_ffn, ffn_w_up, ffn_conv_w, ffn_conv_b, ffn_w_down, norm_final):
    B, S, D = x_prompt.shape
    Bs, Ls, _ = x_sample.shape
    depth = w_in.shape[0]
    n_meta = meta_tokens.shape[0]
    W = lru_conv_w.shape[2]
    CD = ssd_conv_w.shape[2]
    SI = HEADS * HEAD_P
    F2 = ffn_w_up.shape[2]
    w = _prep_weights(norm_mix, w_in, lru_conv_w, lru_conv_b, lru_wa, lru_ba, lru_wx, lru_bx, lru_a_param,
                      lru_out_norm, ssd_conv_w, ssd_conv_b, ssd_dt_bias, ssd_a_log, ssd_d, ssd_out_norm, w_out,
                      norm_ffn, ffn_w_up, ffn_conv_w, ffn_conv_b, ffn_w_down)
    gfin = norm_final.reshape(1, D).astype(f32)

    xm = jnp.broadcast_to(meta_tokens.astype(f32)[:, None, :], (n_meta, B, D)).reshape(n_meta * B, D)
    zero = dict(lc=jnp.zeros((1, 3 * B, W), f32), lh=jnp.zeros((1, B, W), f32), sc=jnp.zeros((1, 3 * B, CD), f32),
                fc=jnp.zeros((1, 2 * B, F2), f32), ss=jnp.zeros((1, B, SI, SSD_N), f32))
    stack = lambda per_layer, k: jnp.stack([d[k] for d in per_layer])
    meta_new, meta_ss = [], jnp.zeros((depth, B, SI, SSD_N), f32)
    for l in range(depth):
        xm, new, meta_ss = _short_layer(xm, w, l, zero, 0, meta_ss, gfin, L=n_meta, Bn=B, final=False)
        meta_new.append(new)

    st = dict(lc=_from_tm(stack(meta_new, "lc"), B), lh=stack(meta_new, "lh").reshape(depth, B, 1, W),
              sc=_from_tm(stack(meta_new, "sc"), B), ss=meta_ss)
    meta_fc = stack(meta_new, "fc")
    Tm = _tile(S, 512)
    TTf = 512 // B if S % (512 // B) == 0 else S
    xp = x_prompt
    p_lc, p_lh, p_sc, p_ss, p_fc = [], [], [], [], []
    for l in range(depth):
        xp, lc, lh, sc, ss = _p_mixer(xp, w, l, st, T=Tm, NB=2)
        xp, fc = _p_ffn(xp, w, l, meta_fc, gfin, TT=TTf, final=(l == depth - 1))
        p_lc.append(lc); p_lh.append(lh); p_sc.append(sc); p_ss.append(ss); p_fc.append(fc)

    xs = _to_tm(x_sample)
    st = dict(lc=_to_tm(state_lru_conv), lh=state_lru_h, sc=_to_tm(state_ssd_conv), fc=_to_tm(state_ffn_conv),
              ss=state_ssd.reshape(depth, Bs, SI, SSD_N))
    sample_new, s_ss = [], jnp.zeros((depth, Bs, SI, SSD_N), f32)
    for l in range(depth):
        xs, new, s_ss = _short_layer(xs, w, l, st, l, s_ss, gfin, L=Ls, Bn=Bs, final=(l == depth - 1))
        sample_new.append(new)

    jst = jnp.stack
    return (xp, _from_tm(xs, Bs), jst(p_lc), jst(p_lh).reshape(depth, B, W), jst(p_sc),
            jst(p_ss).reshape(depth, B, HEADS, HEAD_P, SSD_N), _from_tm(jst(p_fc), B),
            _from_tm(stack(sample_new, "lc"), Bs), stack(sample_new, "lh"), _from_tm(stack(sample_new, "sc"), Bs),
            s_ss.reshape(depth, Bs, HEADS, HEAD_P, SSD_N), _from_tm(stack(sample_new, "fc"), Bs))
```

```python
import functools

import jax
import jax.numpy as jnp
from jax import lax
from jax.experimental import pallas as pl
from jax.experimental.pallas import tpu as pltpu

f32 = jnp.float32
bf16 = jnp.bfloat16

EPS = 1e-6
LRU_C = 8.0
HEADS = 8
HEAD_P = 64
SSD_N = 128
ROW_TILE = 512
LANES = 128
SUBLANES = 8
VMEM_LIMIT = 58 * 1024 * 1024
HI = lax.Precision.HIGHEST


def _rms(x, g):
    return x * lax.rsqrt(jnp.mean(x * x, axis=-1, keepdims=True) + EPS) * g


def _silu(x):
    return x * jax.nn.sigmoid(x)


def _softplus(x):
    return jnp.maximum(x, 0.0) + jnp.log1p(jnp.exp(-jnp.abs(x)))


def _dot(a, b):
    return jnp.dot(a, b, preferred_element_type=f32)


def _lru_coeffs(xc, pre_r, pre_i, a_param):
    r = jax.nn.sigmoid(pre_r)
    i = jax.nn.sigmoid(pre_i)
    log_a = -LRU_C * r * _softplus(-a_param)
    a = jnp.exp(log_a)
    th = jnp.tanh(log_a)
    mult = jnp.sqrt(-2.0 * th / (1.0 - th))
    return a, mult * (i * xc)


def _gates(xc, wg_ref, bg_ref):
    half = wg_ref.shape[1]
    pr, pi = [], []
    for j in range(wg_ref.shape[0]):
        ri = _dot(xc[:, j * half:(j + 1) * half].astype(bf16), wg_ref[j])
        pr.append(ri[:, :half])
        pi.append(ri[:, half:])
    bg = bg_ref[...]
    return jnp.concatenate(pr, axis=1) + bg[0:1], jnp.concatenate(pi, axis=1) + bg[1:2]


def _head_expand():
    r = lax.broadcasted_iota(jnp.int32, (LANES, HEADS * HEAD_P), 0)
    c = lax.broadcasted_iota(jnp.int32, (LANES, HEADS * HEAD_P), 1)
    return (r == lax.shift_right_logical(c, 6)).astype(f32)


def _layer_spec(a, l):
    nd = a.ndim - 1
    return pl.BlockSpec((None,) + a.shape[1:], lambda *_: (l,) + (0,) * nd, pipeline_mode=pl.Buffered(1))


def _full_spec(a):
    nd = a.ndim
    return pl.BlockSpec(a.shape, lambda *_: (0,) * nd)


MIX_KEYS = ("gmix", "win", "lcw", "lcb", "wg", "bg", "ap", "lnorm", "scw", "scb", "dtb", "alog", "dvec")


def _ssd_chunk(xb_ref, dt_ref, s_ref, y_ref, a_row, eexp, dvec):
    xb = xb_ref[...]
    dt_c = dt_ref[...]
    Q = xb.shape[0]
    SI = HEADS * HEAD_P
    row = lax.broadcasted_iota(jnp.int32, (Q, Q), 0)
    col = lax.broadcasted_iota(jnp.int32, (Q, Q), 1)
    causal = col <= row
    lane_x = lax.broadcasted_iota(jnp.int32, (Q, LANES), 1) < HEAD_P
    lane_s = lax.broadcasted_iota(jnp.int32, (SSD_N, LANES), 1) < HEAD_P
    cs_col = jnp.dot(causal.astype(f32), dt_c * a_row, precision=HI, preferred_element_type=f32)
    yield
    cs_row = cs_col.T[0:HEADS]
    dt_row = dt_c.T[0:HEADS]
    ecol = jnp.exp(cs_col)
    w8 = jnp.exp(cs_row[:, Q - 1:Q] - cs_row) * dt_row
    dec_x = jnp.exp(jnp.dot(cs_col[Q - SUBLANES:Q], eexp, precision=HI,
                            preferred_element_type=f32)[SUBLANES - 1:SUBLANES])
    yield
    for g in range(2):
        bm_t = xb[:, SI + g * SSD_N:SI + (g + 1) * SSD_N].T
        cm = xb[:, SI + 2 * SSD_N + g * SSD_N:SI + 2 * SSD_N + (g + 1) * SSD_N]
        scores = _dot(cm.astype(bf16), bm_t.astype(bf16))
        yield
        for jj in range(2):
            j = 2 * g + jj
            h0, h1 = 2 * j, 2 * j + 1
            m = [scores * jnp.where(causal, jnp.exp(cs_col[:, h:h + 1] - cs_row[h:h + 1]), 0.0) * dt_row[h:h + 1]
                 for h in (h0, h1)]
            lhs = jnp.concatenate([cm * ecol[:, h0:h0 + 1], cm * ecol[:, h1:h1 + 1], m[0], m[1]], axis=1)
            xp = xb[:, j * LANES:(j + 1) * LANES]
            sp = s_ref[j]
            xl = jnp.where(lane_x, xp, 0.0)
            xr = jnp.where(lane_x, 0.0, xp)
            rhs = jnp.concatenate([jnp.where(lane_s, sp, 0.0), jnp.where(lane_s, 0.0, sp), xl, xr], axis=0)
            y = _dot(lhs.astype(bf16), rhs.astype(bf16))
            upd = _dot(jnp.concatenate([bm_t * w8[h0:h0 + 1], bm_t * w8[h1:h1 + 1]], axis=1).astype(bf16),
                       jnp.concatenate([xl, xr], axis=0).astype(bf16))
            s_ref[j] = dec_x[:, j * LANES:(j + 1) * LANES] * sp + upd
            y_ref[:, j * LANES:(j + 1) * LANES] = y + xp * dvec[:, j * LANES:(j + 1) * LANES]
            yield


def _p_mixer_tm_kernel(x_ref, xn_ref, gmix_ref, win_ref, lcw_ref, lcb_ref, wg_ref, bg_ref, ap_ref, lnorm_ref,
                       scw_ref, scb_ref, dtb_ref, alog_ref, dvec_ref, snorm_ref, wout_ref,
                       st_lc_ref, st_lh_ref, st_sc_ref, st_ss_ref,
                       o_ref, n_lc_ref, n_lh_ref, n_sc_ref, n_ss_ref,
                       hn_sc, pl_sc, xbc_sc, gz_sc, dt_sc, mix_sc, cl_sc, cs_sc, h_sc, s_pair,
                       xbb_sc, dtb_sc, ysb_sc, *, TT, B, NH, in_bm):
    i = pl.program_id(0)
    R = TT * B
    W = cl_sc.shape[1]
    CD = cs_sc.shape[1]
    SI = HEADS * HEAD_P
    NP = HEADS // 2

    @pl.when(i == 0)
    def _():
        cl_sc[...] = st_lc_ref[...]
        cs_sc[...] = st_sc_ref[...]
        h_sc[...] = st_lh_ref[...]
        for b in range(B):
            for j in range(NP):
                s_pair[b, j] = st_ss_ref[b, j * LANES:(j + 1) * LANES, :].T

    a_row = -jnp.exp(alog_ref[...])
    eexp = _head_expand()
    dvec = dvec_ref[...]

    def rows(ref, k):
        if in_bm:
            return jnp.swapaxes(ref[:, k * TT:(k + 1) * TT, :], 0, 1).reshape(R, ref.shape[2])
        return ref[k * R:(k + 1) * R, :]

    def stage_a(s, src_ref, k):
        def norm():
            hn_sc[...] = _rms(rows(src_ref, k), gmix_ref[...]).astype(bf16)

        def proj(dst, d0, c0, n):
            def step():
                dst[s, :, d0:d0 + n] = _dot(hn_sc[...], win_ref[:, c0:c0 + n])
            return step

        def dt():
            dt_sc[s] = _softplus(_dot(hn_sc[...], win_ref[:, CD + 3 * W:CD + 3 * W + LANES]) + dtb_ref[...])

        steps = [norm, proj(pl_sc, 0, 0, W)]
        steps += [proj(xbc_sc, c, 3 * W + c, W) for c in range(0, CD, W)]
        steps += [proj(gz_sc, 0, W, W), proj(gz_sc, W, 2 * W, W), dt]
        return steps

    def lru_branch(s):
        xc, cl_new = _tm_conv(cl_sc[...], pl_sc[s], lcw_ref[...], lcb_ref[...], B, TT)
        cl_sc[...] = cl_new
        pre_r, pre_i = _gates(xc, wg_ref, bg_ref)
        a, bt = _lru_coeffs(xc, pre_r, pre_i, ap_ref[...])
        yield
        hp = h_sc[...]
        hs = []
        for t in range(TT):
            hp = a[t * B:(t + 1) * B] * hp + bt[t * B:(t + 1) * B]
            hs.append(hp)
            if t % 8 == 7:
                yield
        h_sc[...] = hp
        mix_sc[:, 0:W] = _rms(jnp.concatenate(hs, axis=0) * jax.nn.gelu(gz_sc[s, :, 0:W]),
                              lnorm_ref[...]).astype(bf16)

    def ssd_branch(s):
        xb, cs_new = _tm_conv(cs_sc[...], xbc_sc[s], scw_ref[...], scb_ref[...], B, TT)
        cs_sc[...] = cs_new
        xbb_sc[...] = jnp.swapaxes(_silu(xb).reshape(TT, B, CD), 0, 1)
        dtb_sc[...] = jnp.swapaxes(dt_sc[s].reshape(TT, B, LANES), 0, 1)
        yield
        chains = [_ssd_chunk(xbb_sc.at[b], dtb_sc.at[b], s_pair.at[b], ysb_sc.at[b], a_row, eexp, dvec)
                  for b in range(B)]
        while chains:
            chains = [c for c in chains if next(c, True) is None]
            yield
        ys = jnp.swapaxes(ysb_sc[...], 0, 1).reshape(R, SI)
        mix_sc[:, W:W + SI] = _rms(ys * _silu(gz_sc[s, :, W:2 * W]), snorm_ref[...]).astype(bf16)

    def stage_b(s):
        branches = [lru_branch(s), ssd_branch(s)]
        while branches:
            branches = [g for g in branches if next(g, True) is None]
            yield

    def stage_c(s, k):
        o_ref[k * R:(k + 1) * R, :] = rows(x_ref, k) + _dot(mix_sc[...], wout_ref[...])

    @pl.when(i == 0)
    def _():
        for step in stage_a(0, x_ref, 0):
            step()

    for k in range(NH):
        s = k % 2
        pending = stage_a(1 - s, x_ref, k + 1) if k + 1 < NH else stage_a(1 - s, xn_ref, 0)
        for _ in stage_b(s):
            if pending:
                pending.pop(0)()
        for step in pending:
            step()
        stage_c(s, k)

    @pl.when(i == pl.num_programs(0) - 1)
    def _():
        n_lc_ref[...] = cl_sc[...]
        n_sc_ref[...] = cs_sc[...]
        n_lh_ref[...] = h_sc[...]
        for b in range(B):
            for j in range(NP):
                n_ss_ref[b, j * LANES:(j + 1) * LANES, :] = s_pair[b, j].T


def _p_mixer_tm(x, w, l, st, *, TT, NH, B):
    in_bm = x.ndim == 3
    D = x.shape[-1]
    S = x.shape[1] if in_bm else x.shape[0] // B
    W = w["lcw"].shape[-1]
    CD = w["scw"].shape[-1]
    SI = HEADS * HEAD_P
    assert NH % 2 == 0 and B % SUBLANES == 0
    R = TT * B
    n = S // (NH * TT)
    consts = [w[k] for k in MIX_KEYS] + [w["snorm"], w["wout"]]
    if in_bm:
        x_spec = pl.BlockSpec((B, NH * TT, D), lambda i: (0, i, 0))
        xn_spec = pl.BlockSpec((B, TT, D), lambda i: (0, jnp.minimum(i + 1, n - 1) * NH, 0))
    else:
        x_spec = pl.BlockSpec((NH * R, D), lambda i: (i, 0))
        xn_spec = pl.BlockSpec((R, D), lambda i: (jnp.minimum(i + 1, n - 1) * NH, 0))
    st_shapes = [(3 * B, W), (B, W), (3 * B, CD), (B, SI, SSD_N)]
    full = lambda s: pl.BlockSpec(s, lambda i: (0,) * len(s))
    vm = pltpu.VMEM
    return pl.pallas_call(
        functools.partial(_p_mixer_tm_kernel, TT=TT, B=B, NH=NH, in_bm=in_bm),
        grid=(n,),
        in_specs=[x_spec, xn_spec] + [_layer_spec(c, l) for c in consts]
                 + [_layer_spec(st[k], l) for k in ("lc", "lh", "sc", "ss")],
        out_specs=[pl.BlockSpec((NH * R, D), lambda i: (i, 0))] + [full(s) for s in st_shapes],
        out_shape=[jax.ShapeDtypeStruct((S * B, D), f32)] + [jax.ShapeDtypeStruct(s, f32) for s in st_shapes],
        scratch_shapes=[vm((R, D), bf16), vm((2, R, W), f32), vm((2, R, CD), f32),
                        vm((2, R, 2 * W), f32), vm((2, R, LANES), f32), vm((R, W + SI), bf16),
                        vm((3 * B, W), f32), vm((3 * B, CD), f32), vm((B, W), f32),
                        vm((B, HEADS // 2, SSD_N, LANES), f32),
                        vm((B, TT, CD), f32), vm((B, TT, LANES), f32), vm((B, TT, SI), f32)],
        compiler_params=pltpu.CompilerParams(dimension_semantics=("arbitrary",), vmem_limit_bytes=VMEM_LIMIT),
        name="p_mixer",
    )(x, x, *consts, st["lc"], st["lh"], st["sc"], st["ss"])


def _p_ffn_kernel(x_ref, g_ref, wup_ref, cw_ref, cb_ref, wdn_ref, gfin_ref, st_ref,
                  o_ref, nst_ref, carry, *, TT, B, final):
    i = pl.program_id(0)
    R = TT * B
    D = x_ref.shape[-1]
    F = wdn_ref.shape[0]

    @pl.when(i == 0)
    def _():
        carry[...] = st_ref[...]

    x = x_ref[...]
    hn = _rms(x, g_ref[...]).astype(bf16)
    ys = []
    for half in range(2):
        c0 = half * F
        u = _dot(hn, wup_ref[:, c0:c0 + F])
        ext = jnp.concatenate([carry[:, c0:c0 + F], u], axis=0)
        cw = cw_ref[:, c0:c0 + F]
        ys.append(cb_ref[:, c0:c0 + F] + cw[0:1] * ext[0:R] + cw[1:2] * ext[B:B + R] + cw[2:3] * u)
        carry[:, c0:c0 + F] = ext[R:R + 2 * B]
    act = (jax.nn.gelu(ys[0]) * ys[1]).astype(bf16)
    out = x + _dot(act, wdn_ref[...])
    if final:
        o_ref[...] = jnp.swapaxes(_rms(out, gfin_ref[...]).reshape(TT, B, D), 0, 1)
    else:
        o_ref[...] = out

    @pl.when(i == pl.num_programs(0) - 1)
    def _():
        nst_ref[...] = carry[...]


def _p_ffn(x, w, l, st_fc, gfin, *, TT, B, final):
    SB, D = x.shape
    S = SB // B
    F2 = w["wup"].shape[-1]
    assert B % SUBLANES == 0, "time-major row tiles need whole sublane groups per time step"
    consts = [w["gffn"], w["wup"], w["cw"], w["cb"], w["wdn"]]
    x_spec = pl.BlockSpec((TT * B, D), lambda i: (i, 0))
    if final:
        o_spec, o_shape = pl.BlockSpec((B, TT, D), lambda i: (0, i, 0)), (B, S, D)
    else:
        o_spec, o_shape = x_spec, (SB, D)
    return pl.pallas_call(
        functools.partial(_p_ffn_kernel, TT=TT, B=B, final=final),
        grid=(S // TT,),
        in_specs=[x_spec] + [_layer_spec(c, l) for c in consts] + [_full_spec(gfin), _layer_spec(st_fc, l)],
        out_specs=[o_spec, pl.BlockSpec((2 * B, F2), lambda i: (0, 0))],
        out_shape=[jax.ShapeDtypeStruct(o_shape, f32), jax.ShapeDtypeStruct((2 * B, F2), f32)],
        scratch_shapes=[pltpu.VMEM((2 * B, F2), f32)],
        compiler_params=pltpu.CompilerParams(dimension_semantics=("arbitrary",), vmem_limit_bytes=VMEM_LIMIT),
        name="p_ffn",
    )(x, *consts, gfin, st_fc)


def _tm_conv(state, u, w, b, n_rows, L):
    K = w.shape[0]
    ext = jnp.concatenate([state, u], axis=0)
    R = L * n_rows
    y = b
    for k in range(K):
        y = y + w[k:k + 1] * ext[k * n_rows:k * n_rows + R]
    return y, ext[R:R + (K - 1) * n_rows]


def _s_mix_a_kernel(x_ref, gmix_ref, win_ref, lcw_ref, lcb_ref, wg_ref, bg_ref, ap_ref, lnorm_ref,
                    scw_ref, scb_ref, dtb_ref, alog_ref, dvec_ref,
                    st_lc_ref, st_lh_ref, st_sc_ref,
                    lru_ref, z_ref, bc_ref, ypart_ref, ecs_ref, xw_ref, dec_ref,
                    n_lc_ref, n_lh_ref, n_sc_ref, *, L, Bn):
    W = lcw_ref.shape[1]
    CD = scw_ref.shape[1]
    SI = HEADS * HEAD_P
    hn = _rms(x_ref[...], gmix_ref[...]).astype(bf16)

    xc, n_lc = _tm_conv(st_lc_ref[...], _dot(hn, win_ref[:, 0:W]), lcw_ref[...], lcb_ref[...], Bn, L)
    n_lc_ref[...] = n_lc
    pre_r, pre_i = _gates(xc, wg_ref, bg_ref)
    a, bt = _lru_coeffs(xc, pre_r, pre_i, ap_ref[...])
    hp = st_lh_ref[...]
    hs = []
    for t in range(L):
        hp = a[t * Bn:(t + 1) * Bn] * hp + bt[t * Bn:(t + 1) * Bn]
        hs.append(hp)
    n_lh_ref[...] = hp
    gate = _dot(hn, win_ref[:, W:2 * W])
    lru_ref[...] = _rms(jnp.concatenate(hs, axis=0) * jax.nn.gelu(gate), lnorm_ref[...])
    z_ref[...] = _dot(hn, win_ref[:, 2 * W:3 * W])

    xb, n_sc = _tm_conv(st_sc_ref[...], _dot(hn, win_ref[:, 3 * W:3 * W + CD]), scw_ref[...], scb_ref[...], Bn, L)
    n_sc_ref[...] = n_sc
    xb = _silu(xb)
    xs = xb[:, 0:SI]
    R = L * Bn
    bc_ref[0:R, :] = xb[:, SI:CD]
    if bc_ref.shape[0] > R:
        bc_ref[R:, :] = jnp.zeros((bc_ref.shape[0] - R, CD - SI), f32)
        xw_ref[R:, :] = jnp.zeros((xw_ref.shape[0] - R, SI), f32)
    dt = _softplus(_dot(hn, win_ref[:, CD + 3 * W:CD + 3 * W + LANES]) + dtb_ref[...])
    dA = dt * (-jnp.exp(alog_ref[...]))
    css = []
    cs = jnp.zeros((Bn, LANES), f32)
    for t in range(L):
        cs = cs + dA[t * Bn:(t + 1) * Bn]
        css.append(cs)
    dec_ref[...] = jnp.exp(cs)
    eexp = _head_expand()
    csx = jnp.dot(jnp.concatenate(css, axis=0), eexp, precision=HI, preferred_element_type=f32)
    dtx = jnp.dot(dt, eexp, precision=HI, preferred_element_type=f32)
    ecs_ref[...] = jnp.exp(csx)
    cs_end = csx[(L - 1) * Bn:L * Bn]
    dvec = dvec_ref[...]
    half = SI // 2
    for t in range(L):
        rt = slice(t * Bn, (t + 1) * Bn)
        xw_ref[rt, :] = xs[rt] * jnp.exp(cs_end - csx[rt]) * dtx[rt]
        acc = dvec * xs[rt]
        cm = xb[rt, SI + 2 * SSD_N:CD]
        for s in range(t + 1):
            rs = slice(s * Bn, (s + 1) * Bn)
            prod = cm * xb[rs, SI:SI + 2 * SSD_N]
            gx = jnp.concatenate(
                [jnp.broadcast_to(jnp.sum(prod[:, 0:SSD_N], axis=-1, keepdims=True), (Bn, half)),
                 jnp.broadcast_to(jnp.sum(prod[:, SSD_N:2 * SSD_N], axis=-1, keepdims=True), (Bn, half))], axis=1)
            acc = acc + gx * jnp.exp(csx[rt] - csx[rs]) * dtx[rs] * xs[rs]
        ypart_ref[rt, :] = acc


def _s_mix_a(x, w, l, st, ls, *, L, Bn):
    R, D = x.shape
    W = w["lcw"].shape[-1]
    CD = w["scw"].shape[-1]
    SI = HEADS * HEAD_P
    Rp = -(-L // SUBLANES) * SUBLANES * Bn
    sds = jax.ShapeDtypeStruct
    consts = [w[k] for k in MIX_KEYS]
    states = [st["lc"], st["lh"], st["sc"]]
    out_shape = [sds((R, W), f32), sds((R, SI), f32), sds((Rp, CD - SI), f32), sds((R, SI), f32),
                 sds((R, SI), f32), sds((Rp, SI), f32), sds((Bn, LANES), f32),
                 sds((3 * Bn, W), f32), sds((Bn, W), f32), sds((3 * Bn, CD), f32)]
    return pl.pallas_call(
        functools.partial(_s_mix_a_kernel, L=L, Bn=Bn),
        grid=(1,),
        in_specs=[_full_spec(x)] + [_layer_spec(c, l) for c in consts] + [_layer_spec(a, ls) for a in states],
        out_specs=[_full_spec(o) for o in out_shape],
        out_shape=out_shape,
        compiler_params=pltpu.CompilerParams(dimension_semantics=("arbitrary",), vmem_limit_bytes=VMEM_LIMIT),
        name="s_mix_a",
    )(x, *consts, *states)


def _s_ssd_kernel(dec_ref, xw_ref, bc_ref, st_ref, _, yraw_ref, nst_ref, *, bb):
    i = pl.program_id(0)
    GW = (HEADS // 2) * HEAD_P
    xw_all = jnp.swapaxes(xw_ref[...], 0, 1).astype(bf16)
    bc_all = jnp.swapaxes(bc_ref[...], 0, 1).astype(bf16)
    yraw = []
    for j in range(bb):
        xw = xw_all[j]
        bc = bc_all[j]
        ys = []
        for g in range(2):
            sg = st_ref[j, g * GW:(g + 1) * GW, :]
            upd = lax.dot_general(xw[:, g * GW:(g + 1) * GW], bc[:, g * SSD_N:(g + 1) * SSD_N],
                                  (((0,), (0,)), ((), ())), preferred_element_type=f32)
            cg = bc[:, 2 * SSD_N + g * SSD_N:2 * SSD_N + (g + 1) * SSD_N]
            ys.append(lax.dot_general(cg, sg.astype(bf16), (((1,), (1,)), ((), ())),
                                      preferred_element_type=f32))
            for hh in range(HEADS // 2):
                h = g * (HEADS // 2) + hh
                r0 = g * GW + hh * HEAD_P
                nst_ref[j, r0:r0 + HEAD_P, :] = (dec_ref[(i * bb + j) * HEADS + h] * sg[hh * HEAD_P:(hh + 1) * HEAD_P, :]
                                                 + upd[hh * HEAD_P:(hh + 1) * HEAD_P, :])
        yraw.append(jnp.concatenate(ys, axis=1))
    yraw_ref[...] = jnp.swapaxes(jnp.stack(yraw), 0, 1)


def _s_ssd(dec, xw, bc, st_all, l_in, out_all, l_out, *, bb):
    Lp, Bn, SI = xw.shape
    blk3 = lambda c: pl.BlockSpec((Lp, bb, c), lambda i: (0, i, 0))
    st_in = pl.BlockSpec((None, bb, SI, SSD_N), lambda i: (l_in, i, 0, 0))
    st_out = pl.BlockSpec((None, bb, SI, SSD_N), lambda i: (l_out, i, 0, 0))
    return pl.pallas_call(
        functools.partial(_s_ssd_kernel, bb=bb),
        grid=(Bn // bb,),
        in_specs=[pl.BlockSpec(memory_space=pltpu.SMEM), blk3(SI), blk3(bc.shape[2]), st_in,
                  pl.BlockSpec(memory_space=pl.ANY)],
        out_specs=[blk3(SI), st_out],
        out_shape=[jax.ShapeDtypeStruct((Lp, Bn, SI), f32), jax.ShapeDtypeStruct(out_all.shape, f32)],
        input_output_aliases={4: 1},
        compiler_params=pltpu.CompilerParams(dimension_semantics=("arbitrary",), vmem_limit_bytes=VMEM_LIMIT),
        name="s_ssd",
    )(dec, xw, bc, st_all, out_all)


def _s_mix_c_kernel(x_ref, lru_ref, ypart_ref, yraw_ref, ecs_ref, z_ref, snorm_ref, wout_ref, o_ref):
    W = lru_ref.shape[1]
    SI = ypart_ref.shape[1]
    ys = ypart_ref[...] + yraw_ref[...] * ecs_ref[...]
    ssd_out = _rms(ys * _silu(z_ref[...]), snorm_ref[...]).astype(bf16)
    o_ref[...] = (x_ref[...] + _dot(lru_ref[...].astype(bf16), wout_ref[0:W, :])
                  + _dot(ssd_out, wout_ref[W:W + SI, :]))


def _s_mix_c(x, lru, ypart, yraw, ecs, z, w, l):
    acts = [x, lru, ypart, yraw, ecs, z]
    lead = lambda a: pl.BlockSpec((x.shape[0], a.shape[1]), lambda *_: (0, 0))
    return pl.pallas_call(
        _s_mix_c_kernel,
        grid=(1,),
        in_specs=[lead(a) for a in acts] + [_layer_spec(w["snorm"], l), _layer_spec(w["wout"], l)],
        out_specs=_full_spec(x),
        out_shape=jax.ShapeDtypeStruct(x.shape, f32),
        compiler_params=pltpu.CompilerParams(dimension_semantics=("arbitrary",), vmem_limit_bytes=VMEM_LIMIT),
        name="s_mix_c",
    )(*acts, w["snorm"], w["wout"])


def _s_ffn_kernel(x_ref, g_ref, wug_ref, wuv_ref, cwg_ref, cwv_ref, cbg_ref, cbv_ref, wdn_ref, gfin_ref,
                  stg_ref, stv_ref, o_ref, nstg_ref, nstv_ref, hn_sc, acc_sc, *, L, Bn, final):
    c = pl.program_id(0)

    @pl.when(c == 0)
    def _():
        hn_sc[...] = _rms(x_ref[...], g_ref[...]).astype(bf16)
        acc_sc[...] = jnp.zeros(acc_sc.shape, f32)

    hn = hn_sc[...]
    yg, nstg = _tm_conv(stg_ref[...], _dot(hn, wug_ref[...]), cwg_ref[...], cbg_ref[...], Bn, L)
    yv, nstv = _tm_conv(stv_ref[...], _dot(hn, wuv_ref[...]), cwv_ref[...], cbv_ref[...], Bn, L)
    nstg_ref[...] = nstg
    nstv_ref[...] = nstv
    acc_sc[...] += _dot((jax.nn.gelu(yg) * yv).astype(bf16), wdn_ref[...])

    @pl.when(c == pl.num_programs(0) - 1)
    def _():
        out = x_ref[...] + acc_sc[...]
        if final:
            out = _rms(out, gfin_ref[...])
        o_ref[...] = out


def _s_ffn(x, w, l, st_fc, ls, gfin, *, L, Bn, FC, final):
    R, D = x.shape
    F = w["wdn"].shape[1]
    NC = F // FC
    lcolg = lambda rows, k=l: pl.BlockSpec((None, rows, FC), lambda c: (k, 0, c))
    lcolv = lambda rows, k=l: pl.BlockSpec((None, rows, FC), lambda c: (k, 0, c + NC))
    colg = lambda rows: pl.BlockSpec((rows, FC), lambda c: (0, c))
    return pl.pallas_call(
        functools.partial(_s_ffn_kernel, L=L, Bn=Bn, final=final),
        grid=(NC,),
        in_specs=[_full_spec(x), pl.BlockSpec((None, 1, D), lambda c: (l, 0, 0)),
                  lcolg(D), lcolv(D), lcolg(3), lcolv(3), lcolg(1), lcolv(1),
                  pl.BlockSpec((None, FC, D), lambda c: (l, c, 0)), _full_spec(gfin),
                  lcolg(2 * Bn, ls), lcolv(2 * Bn, ls)],
        out_specs=[_full_spec(x), colg(2 * Bn), colg(2 * Bn)],
        out_shape=[jax.ShapeDtypeStruct((R, D), f32), jax.ShapeDtypeStruct((2 * Bn, F), f32),
                   jax.ShapeDtypeStruct((2 * Bn, F), f32)],
        scratch_shapes=[pltpu.VMEM((R, D), bf16), pltpu.VMEM((R, D), f32)],
        compiler_params=pltpu.CompilerParams(dimension_semantics=("arbitrary",), vmem_limit_bytes=VMEM_LIMIT),
        name="s_ffn",
    )(x, w["gffn"], w["wup"], w["wup"], w["cw"], w["cw"], w["cb"], w["cb"], w["wdn"], gfin, st_fc, st_fc)


def _to_tm(a):
    *lead, Bn, K, C = a.shape
    return jnp.swapaxes(a, -3, -2).reshape(*lead, K * Bn, C)


def _from_tm(a, Bn):
    *lead, R, C = a.shape
    return jnp.swapaxes(a.reshape(*lead, R // Bn, Bn, C), -3, -2)


def _short_layer(x, w, l, st, ls, ss_out, gfin, *, L, Bn, final):
    (lru, z, bc, ypart, ecs, xw, dec, n_lc, n_lh, n_sc) = _s_mix_a(x, w, l, st, ls, L=L, Bn=Bn)
    tm3 = lambda a: a.reshape(a.shape[0] // Bn, Bn, a.shape[1])
    yraw, ss_out = _s_ssd(dec[:, :HEADS].reshape(-1), tm3(xw), tm3(bc), st["ss"], ls, ss_out, l, bb=SUBLANES)
    x = _s_mix_c(x, lru, ypart, yraw.reshape(-1, yraw.shape[2]), ecs, z, w, l)
    x, nfg, nfv = _s_ffn(x, w, l, st["fc"], ls, gfin, L=L, Bn=Bn, FC=512, final=final)
    new = dict(lc=n_lc, lh=n_lh, sc=n_sc, fc=jnp.concatenate([nfg, nfv], axis=1))
    return x, new, ss_out


def _prep_weights(norm_mix, w_in, lru_conv_w, lru_conv_b, lru_wa, lru_ba, lru_wx, lru_bx, lru_a_param,
                  lru_out_norm, ssd_conv_w, ssd_conv_b, ssd_dt_bias, ssd_a_log, ssd_d, ssd_out_norm, w_out,
                  norm_ffn, ffn_w_up, ffn_conv_w, ffn_conv_b, ffn_w_down):
    depth = w_in.shape[0]
    W = lru_conv_w.shape[2]
    CD = ssd_conv_w.shape[2]
    SI = HEADS * HEAD_P
    assert SI == W and w_in.shape[2] == 3 * W + CD + HEADS
    win = jnp.pad(w_in, ((0, 0), (0, 0), (0, LANES - HEADS))).astype(bf16)
    hd = W // HEADS
    per = 256 // hd
    nblk = HEADS // per

    def blockdiag(wh):
        wh = wh.reshape(depth, nblk, per, hd, hd)
        eye = jnp.eye(per, dtype=wh.dtype)
        return (eye[None, None, :, None, :, None] * wh[:, :, :, :, None, :]).reshape(depth, nblk, per * hd, per * hd)

    wg = jnp.concatenate([blockdiag(lru_wa), blockdiag(lru_wx)], axis=3).astype(bf16)
    row = lambda v: v.reshape(depth, 1, -1).astype(f32)
    pad_h = lambda v: jnp.pad(v.astype(f32), ((0, 0), (0, LANES - HEADS))).reshape(depth, 1, LANES)
    return dict(
        gmix=row(norm_mix), win=win, lcw=lru_conv_w, lcb=row(lru_conv_b), wg=wg,
        bg=jnp.stack([lru_ba, lru_bx], axis=1), ap=row(lru_a_param), lnorm=row(lru_out_norm),
        scw=ssd_conv_w, scb=row(ssd_conv_b), dtb=pad_h(ssd_dt_bias), alog=pad_h(ssd_a_log),
        dvec=row(jnp.repeat(ssd_d, HEAD_P, axis=1)), snorm=row(ssd_out_norm), wout=w_out.astype(bf16),
        gffn=row(norm_ffn), wup=ffn_w_up.astype(bf16), cw=ffn_conv_w, cb=row(ffn_conv_b),
        wdn=ffn_w_down.astype(bf16))


def kernel(x_prompt, x_sample, state_lru_conv, state_lru_h, state_ssd_conv, state_ssd, state_ffn_conv, meta_tokens, norm_mix, w_in, lru_conv_w, lru_conv_b, lru_wa, lru_ba, lru_wx, lru_bx, lru_a_param, lru_out_norm, ssd_conv_w, ssd_conv_b, ssd_dt_bias, ssd_a_log, ssd_d, ssd_out_norm, w_out, norm_ffn, ffn_w_up, ffn_conv_w, ffn_conv_b, ffn_w_down, norm_final):
    B, S, D = x_prompt.shape
    Bs, Ls, _ = x_sample.shape
    depth = w_in.shape[0]
    n_meta = meta_tokens.shape[0]
    W = lru_conv_w.shape[2]
    CD = ssd_conv_w.shape[2]
    SI = HEADS * HEAD_P
    F2 = ffn_w_up.shape[2]
    w = _prep_weights(norm_mix, w_in, lru_conv_w, lru_conv_b, lru_wa, lru_ba, lru_wx, lru_bx, lru_a_param,
                      lru_out_norm, ssd_conv_w, ssd_conv_b, ssd_dt_bias, ssd_a_log, ssd_d, ssd_out_norm, w_out,
                      norm_ffn, ffn_w_up, ffn_conv_w, ffn_conv_b, ffn_w_down)
    gfin = norm_final.reshape(1, D).astype(f32)

    xm = jnp.broadcast_to(meta_tokens.astype(f32)[:, None, :], (n_meta, B, D)).reshape(n_meta * B, D)
    zero = dict(lc=jnp.zeros((1, 3 * B, W), f32), lh=jnp.zeros((1, B, W), f32), sc=jnp.zeros((1, 3 * B, CD), f32),
                fc=jnp.zeros((1, 2 * B, F2), f32), ss=jnp.zeros((1, B, SI, SSD_N), f32))
    stack = lambda per_layer, k: jnp.stack([d[k] for d in per_layer])
    meta_new, meta_ss = [], jnp.zeros((depth, B, SI, SSD_N), f32)
    for l in range(depth):
        xm, new, meta_ss = _short_layer(xm, w, l, zero, 0, meta_ss, gfin, L=n_meta, Bn=B, final=False)
        meta_new.append(new)

    st = dict(lc=stack(meta_new, "lc"), lh=stack(meta_new, "lh"), sc=stack(meta_new, "sc"), ss=meta_ss)
    meta_fc = stack(meta_new, "fc")
    TTp = ROW_TILE // B if S % (2 * (ROW_TILE // B)) == 0 else S // 2
    xp = x_prompt
    p_lc, p_lh, p_sc, p_ss, p_fc = [], [], [], [], []
    for l in range(depth):
        xp, lc, lh, sc, ss = _p_mixer_tm(xp, w, l, st, TT=TTp, NH=2, B=B)
        xp, fc = _p_ffn(xp, w, l, meta_fc, gfin, TT=TTp, B=B, final=(l == depth - 1))
        p_lc.append(lc); p_lh.append(lh); p_sc.append(sc); p_ss.append(ss); p_fc.append(fc)

    xs = _to_tm(x_sample)
    st = dict(lc=_to_tm(state_lru_conv), lh=state_lru_h, sc=_to_tm(state_ssd_conv), fc=_to_tm(state_ffn_conv),
              ss=state_ssd.reshape(depth, Bs, SI, SSD_N))
    sample_new, s_ss = [], jnp.zeros((depth, Bs, SI, SSD_N), f32)
    for l in range(depth):
        xs, new, s_ss = _short_layer(xs, w, l, st, l, s_ss, gfin, L=Ls, Bn=Bs, final=(l == depth - 1))
        sample_new.append(new)

    jst = jnp.stack
    return (xp, _from_tm(xs, Bs), _from_tm(jst(p_lc), B), jst(p_lh), _from_tm(jst(p_sc), B),
            jst(p_ss).reshape(depth, B, HEADS, HEAD_P, SSD_N), _from_tm(jst(p_fc), B),
            _from_tm(stack(sample_new, "lc"), Bs), stack(sample_new, "lh"), _from_tm(stack(sample_new, "sc"), Bs),
            s_ss.reshape(depth, Bs, HEADS, HEAD_P, SSD_N), _from_tm(stack(sample_new, "fc"), Bs))
```
